```python
import math
import jax, jax.numpy as jnp
from jax import lax
import numpy as np

D_MODEL = 2048
BATCH = 8
SEQ = 2048
DEPTH = 1
DEC_BATCH = 4
DEC_SEQ = 8192
PAST_LEN = 128

MIX_WIDTH = D_MODEL
MLSTM_WIDTH = MIX_WIDTH // 2
ATTN_WIDTH = MIX_WIDTH - MLSTM_WIDTH
MLSTM_HEADS = 4
MLSTM_HD = MLSTM_WIDTH // MLSTM_HEADS
DIFF_HEADS = 8
DIFF_HD = ATTN_WIDTH // DIFF_HEADS // 2
DIFF_VD = 2 * DIFF_HD
D_FF = 4 * D_MODEL
CHUNK = 128
Q_BLOCK = 128
EPS = 1e-6
N_GATE_COLS = 4 * MLSTM_HEADS
IN_WIDTH = 4 * MLSTM_WIDTH + N_GATE_COLS + 2 * DIFF_HEADS * 2 * DIFF_HD + DIFF_HEADS * DIFF_VD

kernel_name = "hybrid_mlstm_diffattn_encoder"


def _rmsnorm(x, g):
    xf = x.astype(jnp.float32)
    y = xf * lax.rsqrt(jnp.mean(xf * xf, axis=-1, keepdims=True) + EPS)
    return (y * g.astype(jnp.float32)).astype(x.dtype)


def _mlstm_chunkwise(q, k, v, log_i, log_f):
    B, H, S, dk = q.shape
    dv = v.shape[-1]
    nc = S // CHUNK

    def to_chunks(a):
        return jnp.moveaxis(a.reshape((B, H, nc, CHUNK) + a.shape[3:]), 2, 0)

    xs = (to_chunks(q), to_chunks(k), to_chunks(v), to_chunks(log_i), to_chunks(log_f))
    lower = jnp.tril(jnp.ones((CHUNK, CHUNK), dtype=bool))

    def step(carry, blk):
        C, n, m = carry
        qb, kb, vb, ib, fb = blk
        b = jnp.cumsum(fb, axis=-1)
        d = b[..., :, None] - b[..., None, :] + ib[..., None, :]
        d = jnp.where(lower, d, -jnp.inf)
        inter = b + m[..., None]
        m_t = jnp.maximum(inter, jnp.max(d, axis=-1))
        w_intra = jnp.exp(d - m_t[..., None])
        w_inter = jnp.exp(inter - m_t)
        s = jnp.einsum('bhtk,bhsk->bhts', qb, kb) * w_intra
        num = jnp.einsum('bhts,bhsv->bhtv', s, vb) + w_inter[..., None] * jnp.einsum('bhtk,bhkv->bhtv', qb, C)
        den = jnp.sum(s, axis=-1) + w_inter * jnp.einsum('bhtk,bhk->bht', qb, n)
        h = num / jnp.maximum(jnp.abs(den), jnp.exp(-m_t))[..., None]
        bL = b[..., -1]
        g = bL[..., None] - b + ib
        m_new = jnp.maximum(bL + m, jnp.max(g, axis=-1))
        wk = jnp.exp(g - m_new[..., None])
        decay = jnp.exp(bL + m - m_new)
        C_new = decay[..., None, None] * C + jnp.einsum('bhs,bhsk,bhsv->bhkv', wk, kb, vb)
        n_new = decay[..., None] * n + jnp.einsum('bhs,bhsk->bhk', wk, kb)
        return (C_new, n_new, m_new), h

    init = (jnp.zeros((B, H, dk, dv), jnp.float32), jnp.zeros((B, H, dk), jnp.float32),
            jnp.zeros((B, H), jnp.float32))
    _, hs = lax.scan(step, init, xs)
    return jnp.moveaxis(hs, 0, 2).reshape(B, H, S, dv)


def _diff_attention(q, k, v, lam, slopes):
    B, S, H, _, d = q.shape
    nb = S // Q_BLOCK
    scale = d ** -0.5
    qb = jnp.moveaxis(q.reshape(B, nb, Q_BLOCK, H, 2, d), 1, 0)
    pos_k = jnp.arange(S)

    def block(args):
        qi, start = args
        s = jnp.einsum('bqhmd,bkhmd->bhmqk', qi, k).astype(jnp.float32) * scale
        pos_q = start + jnp.arange(Q_BLOCK)
        dist = jnp.abs(pos_q[:, None] - pos_k[None, :]).astype(jnp.float32)
        s = s - slopes[:, None, None, None] * dist
        p = jax.nn.softmax(s, axis=-1)
        a = p[:, :, 0] - lam * p[:, :, 1]
        return jnp.einsum('bhqk,bkhv->bqhv', a.astype(v.dtype), v)

    out = lax.map(block, (qb, jnp.arange(nb) * Q_BLOCK))
    return jnp.moveaxis(out, 0, 1).reshape(B, S, H, v.shape[-1])


def _mixer(h, w_in, b_gates, mlstm_norm_g, lambda_q1, lambda_k1, lambda_q2, lambda_k2, subln_g, w_out, lambda_init):
    B, S, _ = h.shape
    f32 = jnp.float32
    proj = jnp.einsum('bsd,de->bse', h, w_in)
    sizes = [MLSTM_WIDTH] * 4 + [MLSTM_HEADS] * 4 + [DIFF_HEADS * 2 * DIFF_HD] * 2 + [DIFF_HEADS * DIFF_VD]
    points = [int(c) for c in np.cumsum(sizes)[:-1]]
    mq, mk, mv, mo, gi_f, gf_f, gi_b, gf_b, aq, ak, av = jnp.split(proj, points, axis=-1)
    bi_f, bf_f, bi_b, bf_b = jnp.split(b_gates, 4)

    def heads(a):
        return a.reshape(B, S, MLSTM_HEADS, MLSTM_HD).transpose(0, 2, 1, 3).astype(f32)

    def gate(a, bias):
        return (a + bias).astype(f32).transpose(0, 2, 1)

    q = heads(mq) * (MLSTM_HD ** -0.5)
    k = heads(mk)
    v = heads(mv)
    log_i_f = gate(gi_f, bi_f)
    log_f_f = jax.nn.log_sigmoid(gate(gf_f, bf_f))
    log_i_b = gate(gi_b, bi_b)
    log_f_b = jax.nn.log_sigmoid(gate(gf_b, bf_b))
    h_fwd = _mlstm_chunkwise(q, k, v, log_i_f, log_f_f)
    rev = lambda a: jnp.flip(a, axis=2)
    h_bwd = rev(_mlstm_chunkwise(rev(q), rev(k), rev(v), rev(log_i_b), rev(log_f_b)))
    hm = (h_fwd + h_bwd).transpose(0, 2, 1, 3)
    hm = _rmsnorm(hm, mlstm_norm_g.reshape(MLSTM_HEADS, MLSTM_HD))
    y_m = (jax.nn.sigmoid(mo.astype(f32)) * hm.reshape(B, S, MLSTM_WIDTH)).astype(h.dtype)

    dq = aq.reshape(B, S, DIFF_HEADS, 2, DIFF_HD)
    dk = ak.reshape(B, S, DIFF_HEADS, 2, DIFF_HD)
    dv = av.reshape(B, S, DIFF_HEADS, DIFF_VD)
    lam = (jnp.exp(jnp.sum(lambda_q1.astype(f32) * lambda_k1.astype(f32)))
           - jnp.exp(jnp.sum(lambda_q2.astype(f32) * lambda_k2.astype(f32))) + lambda_init)
    slopes = 2.0 ** (-8.0 * jnp.arange(1, DIFF_HEADS + 1, dtype=f32) / DIFF_HEADS)
    o = _diff_attention(dq, dk, dv, lam, slopes)
    o = _rmsnorm(o, subln_g) * (1.0 - lambda_init)
    y_a = o.reshape(B, S, ATTN_WIDTH).astype(h.dtype)

    y = jnp.concatenate([y_m, y_a], axis=-1)
    return jnp.einsum('bse,ed->bsd', y, w_out)


def _trunk(x, g_pre_mix, w_in, b_gates, mlstm_norm_g, lambda_q1, lambda_k1, lambda_q2, lambda_k2,
           subln_g, w_out, g_post_mix, g_pre_mlp, w_up, w_down, g_post_mlp):
    for l in range(DEPTH):
        lambda_init = 0.8 - 0.6 * math.exp(-0.3 * l)
        h = _rmsnorm(x, g_pre_mix[l])
        mix = _mixer(h, w_in[l], b_gates[l], mlstm_norm_g[l], lambda_q1[l], lambda_k1[l],
                     lambda_q2[l], lambda_k2[l], subln_g[l], w_out[l], lambda_init)
        x = x + _rmsnorm(mix, g_post_mix[l])
        h = _rmsnorm(x, g_pre_mlp[l])
        u = jax.nn.relu(jnp.einsum('bsd,df->bsf', h, w_up[l]))
        m = jnp.einsum('bsf,fd->bsd', u * u, w_down[l])
        x = x + _rmsnorm(m, g_post_mlp[l])
    return x


def setup_inputs(seed: int = 0) -> dict:
    key = jax.random.key(seed)
    ks = jax.random.split(key, 20)
    f32 = jnp.float32

    def gain(k, n):
        return 1.0 + 0.02 * jax.random.normal(k, (DEPTH, n), f32)

    i_bias = 0.1 * jax.random.normal(ks[0], (DEPTH, MLSTM_HEADS), f32)
    f_bias = jnp.broadcast_to(jnp.linspace(3.0, 6.0, MLSTM_HEADS, dtype=f32), (DEPTH, MLSTM_HEADS)) \
        + 0.1 * jax.random.normal(ks[1], (DEPTH, MLSTM_HEADS), f32)
    i_bias_b = 0.1 * jax.random.normal(ks[2], (DEPTH, MLSTM_HEADS), f32)
    f_bias_b = jnp.broadcast_to(jnp.linspace(3.0, 6.0, MLSTM_HEADS, dtype=f32), (DEPTH, MLSTM_HEADS)) \
        + 0.1 * jax.random.normal(ks[3], (DEPTH, MLSTM_HEADS), f32)
    b_gates = jnp.concatenate([i_bias, f_bias, i_bias_b, f_bias_b], axis=-1)
    return {
        'x_prompt': jax.random.normal(ks[4], (BATCH, SEQ, D_MODEL), f32),
        'x_sample': jax.random.normal(ks[5], (DEC_BATCH, DEC_SEQ, D_MODEL), f32),
        'g_pre_mix': gain(ks[6], D_MODEL),
        'w_in': jax.random.normal(ks[7], (DEPTH, D_MODEL, IN_WIDTH), f32) * D_MODEL ** -0.5,
        'b_gates': b_gates,
        'mlstm_norm_g': gain(ks[8], MLSTM_WIDTH),
        'lambda_q1': 0.1 * jax.random.normal(ks[9], (DEPTH, DIFF_HD), f32),
        'lambda_k1': 0.1 * jax.random.normal(ks[10], (DEPTH, DIFF_HD), f32),
        'lambda_q2': 0.1 * jax.random.normal(ks[11], (DEPTH, DIFF_HD), f32),
        'lambda_k2': 0.1 * jax.random.normal(ks[12], (DEPTH, DIFF_HD), f32),
        'subln_g': gain(ks[13], DIFF_VD),
        'w_out': jax.random.normal(ks[14], (DEPTH, MIX_WIDTH, D_MODEL), f32) * MIX_WIDTH ** -0.5,
        'g_post_mix': gain(ks[15], D_MODEL),
        'g_pre_mlp': gain(ks[16], D_MODEL),
        'w_up': jax.random.normal(ks[17], (DEPTH, D_MODEL, D_FF), f32) * D_MODEL ** -0.5,
        'w_down': jax.random.normal(ks[18], (DEPTH, D_FF, D_MODEL), f32) * D_FF ** -0.5,
        'g_post_mlp': gain(ks[19], D_MODEL),
    }


def reference(x_prompt, x_sample, g_pre_mix, w_in, b_gates, mlstm_norm_g, lambda_q1, lambda_k1,
              lambda_q2, lambda_k2, subln_g, w_out, g_post_mix, g_pre_mlp, w_up, w_down, g_post_mlp):
    y_prompt = _trunk(x_prompt, g_pre_mix, w_in, b_gates, mlstm_norm_g, lambda_q1, lambda_k1, lambda_q2,
                      lambda_k2, subln_g, w_out, g_post_mix, g_pre_mlp, w_up, w_down, g_post_mlp)
    y_sample = _trunk(x_sample, g_pre_mix, w_in, b_gates, mlstm_norm_g, lambda_q1, lambda_k1, lambda_q2,
                      lambda_k2, subln_g, w_out, g_post_mix, g_pre_mlp, w_up, w_down, g_post_mlp)
    return (y_prompt, y_sample)
```

```python
import functools
import math
import struct

import jax
import jax.numpy as jnp
from jax import lax
from jax.experimental import pallas as pl
from jax.experimental.pallas import tpu as pltpu

F32 = jnp.float32
BF16 = jnp.bfloat16

EPS = 1e-6
MLSTM_HEADS = 4
DIFF_HEADS = 8
CHUNK = 128
N_GATES = 4 * MLSTM_HEADS
LANES = 128
NEG_BIG = -1e30
VMEM_LIMIT = 56 * 1024 * 1024

ATTN_TQ = 256
ATTN_TK = 512
ATTN_VROWS = 144
N_FEAT = 6


def _f32_bits(x):
    return struct.unpack("<I", struct.pack("<f", x))[0]


def _round_f32(x):
    return struct.unpack("<f", struct.pack("<f", x))[0]


def _round_bf16(x):
    bits = _f32_bits(x)
    bits = (bits + 0x7FFF + ((bits >> 16) & 1)) & 0xFFFF0000
    return struct.unpack("<f", struct.pack("<I", bits))[0]


def _split3(value):
    parts, rest = [], value
    for _ in range(3):
        parts.append(_round_bf16(rest))
        rest = _round_f32(rest - parts[-1])
    assert rest == 0.0
    return parts


LOG2E = _round_f32(math.log2(math.e))
LOG2E_PARTS = _split3(LOG2E)


def _rms(x, g):
    return x * lax.rsqrt(jnp.mean(x * x, axis=-1, keepdims=True) + EPS) * g


def _log_sigmoid(x):
    return jnp.minimum(x, 0.0) - jnp.log1p(jnp.exp(-jnp.abs(x)))


def _dot(a, b):
    return jnp.dot(a, b, preferred_element_type=F32)


def _dot_nt(a, b):
    return lax.dot_general(a, b, (((1,), (1,)), ((), ())), preferred_element_type=F32)


def _params(n_axes):
    return pltpu.CompilerParams(dimension_semantics=("arbitrary",) * n_axes,
                                vmem_limit_bytes=VMEM_LIMIT)


def _inproj_kernel(x_ref, g_ref, w_ref, wgc_ref, wgt_ref, bc_ref, bt_ref,
                   proj_ref, lg_ref, lgt_ref, hn_ref):
    @pl.when(pl.program_id(1) == 0)
    def _():
        hn = _rms(x_ref[...], g_ref[...]).astype(BF16)
        hn_ref[...] = hn
        gc = _dot(hn, wgc_ref[...]) + bc_ref[...]
        col = lax.broadcasted_iota(jnp.int32, gc.shape, 1)
        lg_ref[...] = jnp.where((col & MLSTM_HEADS) != 0, _log_sigmoid(gc), gc)
        gt = _dot_nt(wgt_ref[...], hn) + bt_ref[...]
        row = lax.broadcasted_iota(jnp.int32, gt.shape, 0)
        lgt_ref[...] = jnp.where((row & MLSTM_HEADS) != 0, _log_sigmoid(gt), gt)

    proj_ref[...] = _dot(hn_ref[...], w_ref[...]).astype(BF16)


def _in_proj(x, g, w_main, w_gc, w_gt, b_c, b_t):
    B, S, D = x.shape
    NM = w_main.shape[1]
    tm = min(1024, S)
    tn = 1024
    nst = S // tm
    return pl.pallas_call(
        _inproj_kernel,
        grid=(B * nst, NM // tn),
        in_specs=[
            pl.BlockSpec((None, tm, D), lambda i, j: (i // nst, i % nst, 0)),
            pl.BlockSpec((1, D), lambda i, j: (0, 0)),
            pl.BlockSpec((D, tn), lambda i, j: (0, j)),
            pl.BlockSpec((D, LANES), lambda i, j: (0, 0)),
            pl.BlockSpec((N_GATES, D), lambda i, j: (0, 0)),
            pl.BlockSpec((1, LANES), lambda i, j: (0, 0)),
            pl.BlockSpec((N_GATES, 1), lambda i, j: (0, 0)),
        ],
        out_specs=[
            pl.BlockSpec((None, tm, tn), lambda i, j: (i // nst, i % nst, j)),
            pl.BlockSpec((None, tm, LANES), lambda i, j: (i // nst, i % nst, 0)),
            pl.BlockSpec((None, N_GATES, tm), lambda i, j: (i // nst, 0, i % nst)),
        ],
        out_shape=[
            jax.ShapeDtypeStruct((B, S, NM), BF16),
            jax.ShapeDtypeStruct((B, S, LANES), F32),
            jax.ShapeDtypeStruct((B, N_GATES, S), F32),
        ],
        scratch_shapes=[pltpu.VMEM((tm, D), BF16)],
        compiler_params=_params(2),
        name="in_proj",
    )(x, g, w_main, w_gc, w_gt, b_c, b_t)


def _split2(a):
    hi = a.astype(BF16)
    lo = (a - hi.astype(F32)).astype(BF16)
    return hi, lo


def _mlstm_direction(d, p_ref, lg_ref, lgt_ref, out_ref, c_ref, n_ref, m_ref, hd):
    L = CHUNK
    r_i = lax.broadcasted_iota(jnp.int32, (L, L), 0)
    c_i = lax.broadcasted_iota(jnp.int32, (L, L), 1)
    low, up = c_i <= r_i, c_i >= r_i
    mask = up if d else low
    m_col = mask.astype(BF16)
    m_row = (low if d else up).astype(BF16)
    lg = lg_ref[...]
    lgt = lgt_ref[...]
    lg_hi, lg_lo = _split2(lg)
    a_all = _dot(m_col, lg_hi) + _dot(m_col, lg_lo)
    lgt_hi, lgt_lo = _split2(lgt)
    b_all = _dot(lgt_hi, m_row) + _dot(lgt_lo, m_row)
    inv_sqrt = hd ** -0.5
    for h in range(MLSTM_HEADS):
        idx = d * MLSTM_HEADS + h
        icol = 2 * d * MLSTM_HEADS + h
        fcol = icol + MLSTM_HEADS
        a_col = a_all[:, fcol:fcol + 1]
        r_row = lgt[icol:icol + 1, :] - b_all[fcol:fcol + 1, :]
        m_prev = m_ref[idx][0:1, 0:1]
        dmat = jnp.where(mask, a_col + r_row, NEG_BIG)
        inter = a_col + m_prev
        m_t = jnp.maximum(inter, jnp.max(dmat, axis=1, keepdims=True))
        w_intra = jnp.exp(dmat - m_t)
        w_inter = jnp.exp(inter - m_t) * inv_sqrt

        qb = p_ref[:, h * hd:(h + 1) * hd]
        kb = p_ref[:, (MLSTM_HEADS + h) * hd:(MLSTM_HEADS + h + 1) * hd]
        vb = p_ref[:, (2 * MLSTM_HEADS + h) * hd:(2 * MLSTM_HEADS + h + 1) * hd]
        s = _dot_nt(qb, kb) * (w_intra * inv_sqrt)
        c_old = c_ref[idx]
        n_old = n_ref[idx]
        num = _dot(s.astype(BF16), vb) + w_inter * _dot(qb, c_old.astype(BF16))
        qn = jnp.sum(qb.astype(F32) * n_old[0:1, :], axis=1, keepdims=True)
        den = jnp.sum(s, axis=1, keepdims=True) + w_inter * qn
        denom = jnp.maximum(jnp.abs(den), jnp.exp(-m_t))
        out_ref[:, h * hd:(h + 1) * hd] = num * (1.0 / denom)

        b_tot = jnp.sum(lgt[fcol:fcol + 1, :], axis=1, keepdims=True)
        g_row = b_tot + r_row
        m_new = jnp.maximum(b_tot + m_prev, jnp.max(g_row, axis=1, keepdims=True))
        wk_row = jnp.exp(g_row - m_new)
        decay = jnp.exp(b_tot + m_prev - m_new)
        ktw = (kb.astype(F32).T * wk_row).astype(BF16)
        c_ref[idx] = decay * c_old + _dot(ktw, vb)
        wk8 = jnp.broadcast_to(wk_row, (8, L)).astype(BF16)
        n_ref[idx] = decay * n_old + _dot(wk8, kb)
        m_ref[idx] = jnp.broadcast_to(m_new, (8, LANES))


def _mlstm_kernel(pf_ref, pb_ref, lgf_ref, lgb_ref, lgtf_ref, lgtb_ref, hf_ref, hb_ref,
                  c_ref, n_ref, m_ref, *, hd):
    @pl.when(pl.program_id(1) == 0)
    def _():
        c_ref[...] = jnp.zeros_like(c_ref)
        n_ref[...] = jnp.zeros_like(n_ref)
        m_ref[...] = jnp.zeros_like(m_ref)

    _mlstm_direction(0, pf_ref, lgf_ref, lgtf_ref, hf_ref, c_ref, n_ref, m_ref, hd)
    _mlstm_direction(1, pb_ref, lgb_ref, lgtb_ref, hb_ref, c_ref, n_ref, m_ref, hd)


def _mlstm(proj, lg, lgt, width):
    B, S, _ = proj.shape
    nc = S // CHUNK
    hd = width // MLSTM_HEADS
    fwd = lambda b, c: (b, c, 0)
    bwd = lambda b, c: (b, nc - 1 - c, 0)
    return pl.pallas_call(
        functools.partial(_mlstm_kernel, hd=hd),
        grid=(B, nc),
        in_specs=[
            pl.BlockSpec((None, CHUNK, 3 * width), fwd),
            pl.BlockSpec((None, CHUNK, 3 * width), bwd),
            pl.BlockSpec((None, CHUNK, LANES), fwd),
            pl.BlockSpec((None, CHUNK, LANES), bwd),
            pl.BlockSpec((None, N_GATES, CHUNK), lambda b, c: (b, 0, c)),
            pl.BlockSpec((None, N_GATES, CHUNK), lambda b, c: (b, 0, nc - 1 - c)),
        ],
        out_specs=[
            pl.BlockSpec((None, CHUNK, width), fwd),
            pl.BlockSpec((None, CHUNK, width), bwd),
        ],
        out_shape=[jax.ShapeDtypeStruct((B, S, width), F32)] * 2,
        scratch_shapes=[
            pltpu.VMEM((2 * MLSTM_HEADS, hd, hd), F32),
            pltpu.VMEM((2 * MLSTM_HEADS, 8, hd), F32),
            pltpu.VMEM((2 * MLSTM_HEADS, 8, LANES), F32),
        ],
        compiler_params=_params(2),
        name="mlstm",
    )(proj, proj, lg, lg, lgt, lgt)


def _attn_kernel(q_ref, k_ref, v_ref, lq1_ref, lk1_ref, lq2_ref, lk2_ref, g_ref, o_ref,
                 ka_ref, vt_ref, qat_ref, acc_ref, m_ref, *, seq, tq, tk, lambda_init):
    head = pl.program_id(1)
    qi = pl.program_id(2)
    nk = seq // tk
    hd = LANES // 2
    slope = pltpu.bitcast(jnp.full((8, LANES), (126 - head) << 23, jnp.int32), F32)[0:1, 0:1]
    c_f32 = slope * LOG2E

    @pl.when(qi == 0)
    def _build_keys():
        def body(j, carry):
            start = pl.multiple_of(j * tk, tk)
            pos = start + lax.broadcasted_iota(jnp.int32, (tk, LANES), 0)
            lane = lax.broadcasted_iota(jnp.int32, (tk, LANES), 1)
            s_lo = pos & (LANES - 1)
            feat = jnp.where(lane < 3, s_lo, jnp.where(lane < N_FEAT, pos - s_lo, 0))
            ka_ref[j, :, 0:LANES] = k_ref[pl.ds(start, tk), :]
            ka_ref[j, :, LANES:2 * LANES] = feat.astype(F32).astype(BF16)
            vt_ref[j, 0:LANES, :] = v_ref[pl.ds(start, tk), :].astype(F32).T.astype(BF16)
            ones_row = lax.broadcasted_iota(jnp.int32, (ATTN_VROWS - LANES, tk), 0) == 0
            vt_ref[j, LANES:ATTN_VROWS, :] = jnp.where(ones_row, 1.0, 0.0).astype(BF16)
            return carry
        lax.fori_loop(0, nk, body, 0)

    q0 = qi * tq
    q_t = q_ref[...].astype(F32).T * (hd ** -0.5 * LOG2E)
    row = lax.broadcasted_iota(jnp.int32, (LANES, tq), 0)
    q_maps = (jnp.where(row < hd, q_t, 0.0).astype(BF16), jnp.where(row >= hd, q_t, 0.0).astype(BF16))
    part = jnp.zeros((LANES, tq), F32)
    for r in range(N_FEAT):
        part = jnp.where(row == r, LOG2E_PARTS[r % 3], part)
    feat_q = part * slope
    feats = (feat_q.astype(BF16), (-feat_q).astype(BF16), jnp.zeros((LANES, tq), BF16))
    for mp in range(2):
        for side in range(3):
            qat_ref[3 * mp + side, 0:LANES, :] = q_maps[mp]
            qat_ref[3 * mp + side, LANES:2 * LANES, :] = feats[side]

    t_pos = (q0 + lax.broadcasted_iota(jnp.int32, (1, tq), 1)).astype(F32)
    ct = c_f32 * t_pos
    r_side = (-ct, ct)

    m_ref[...] = jnp.full(m_ref.shape, NEG_BIG, F32)
    acc_ref[...] = jnp.zeros_like(acc_ref)

    def update(mp, s_t, r, vt):
        m_old = m_ref[mp]
        m_new = jnp.maximum(m_old, jnp.max(s_t, axis=0, keepdims=True) + r)
        p = jnp.exp2(s_t - (m_new - r)).astype(BF16)
        acc_ref[mp] = jnp.exp2(m_old - m_new) * acc_ref[mp] + _dot(vt, p)
        m_ref[mp] = m_new

    def off_diag(side):
        def body(j, carry):
            ka, vt = ka_ref[j], vt_ref[j]
            for mp in range(2):
                update(mp, _dot(ka, qat_ref[3 * mp + side]), r_side[side], vt)
            return carry
        return body

    jd = q0 // tk
    lax.fori_loop(0, jd, off_diag(0), 0)
    s_pos = jd * tk + lax.broadcasted_iota(jnp.int32, (tk, tq), 0)
    t_lane = q0 + lax.broadcasted_iota(jnp.int32, (tk, tq), 1)
    dbias = c_f32 * jnp.abs(s_pos - t_lane).astype(F32)
    ka, vt = ka_ref[jd], vt_ref[jd]
    for mp in range(2):
        update(mp, _dot(ka, qat_ref[3 * mp + 2]) - dbias, jnp.zeros((1, tq), F32), vt)
    lax.fori_loop(jd + 1, nk, off_diag(1), 0)

    lam = (jnp.exp(jnp.sum(lq1_ref[...] * lk1_ref[...], axis=1, keepdims=True))
           - jnp.exp(jnp.sum(lq2_ref[...] * lk2_ref[...], axis=1, keepdims=True)) + lambda_init)
    outs = []
    for mp in range(2):
        a = acc_ref[mp]
        outs.append(a[0:LANES, :] * (1.0 / a[LANES:LANES + 1, :]))
    o_t = outs[0] - lam * outs[1]
    ms = jnp.mean(o_t * o_t, axis=0, keepdims=True)
    y = o_t * lax.rsqrt(ms + EPS) * (g_ref[...] * (1.0 - lambda_init))
    o_ref[...] = y.T.astype(BF16)


def _attn(proj, lq1, lk1, lq2, lk2, g_col, col0, lambda_init):
    B, S, _ = proj.shape
    tq, tk = min(ATTN_TQ, S), min(ATTN_TK, S)
    nk = S // tk
    cb = col0 // LANES
    vec = pl.BlockSpec((1, LANES // 2), lambda b, h, i: (0, 0))
    return pl.pallas_call(
        functools.partial(_attn_kernel, seq=S, tq=tq, tk=tk, lambda_init=lambda_init),
        grid=(B, DIFF_HEADS, S // tq),
        in_specs=[
            pl.BlockSpec((None, tq, LANES), lambda b, h, i: (b, i, cb + h)),
            pl.BlockSpec((None, S, LANES), lambda b, h, i: (b, 0, cb + DIFF_HEADS + h)),
            pl.BlockSpec((None, S, LANES), lambda b, h, i: (b, 0, cb + 2 * DIFF_HEADS + h)),
            vec, vec, vec, vec,
            pl.BlockSpec((LANES, 1), lambda b, h, i: (0, 0)),
        ],
        out_specs=pl.BlockSpec((None, tq, LANES), lambda b, h, i: (b, i, h)),
        out_shape=jax.ShapeDtypeStruct((B, S, DIFF_HEADS * LANES), BF16),
        scratch_shapes=[
            pltpu.VMEM((nk, tk, 2 * LANES), BF16),
            pltpu.VMEM((nk, ATTN_VROWS, tk), BF16),
            pltpu.VMEM((6, 2 * LANES, tq), BF16),
            pltpu.VMEM((2, ATTN_VROWS, tq), F32),
            pltpu.VMEM((2, 1, tq), F32),
        ],
        compiler_params=_params(3),
        name="attn",
    )(proj, proj, proj, lq1, lk1, lq2, lk2, g_col)


def _outproj_kernel(hf_ref, hb_ref, mo_ref, oa_ref, x_ref, gm_ref, w_ref, gp_ref, out_ref, *, hd):
    width = MLSTM_HEADS * hd
    ys = []
    for h in range(MLSTM_HEADS):
        sl = slice(h * hd, (h + 1) * hd)
        hm = _rms(hf_ref[:, sl] + hb_ref[:, sl], gm_ref[:, sl])
        ys.append((jax.nn.sigmoid(mo_ref[:, sl].astype(F32)) * hm).astype(BF16))
    ym = jnp.concatenate(ys, axis=1)
    mix = _dot(ym, w_ref[0:width, :]) + _dot(oa_ref[...], w_ref[width:, :])
    out_ref[...] = x_ref[...] + _rms(mix, gp_ref[...])


def _out_proj(hf, hb, proj, oa, x, gm, w_out, gp, width):
    B, S, D = x.shape
    tm = min(256, S)
    row = lambda b, i: (b, i, 0)
    const = lambda b, i: (0, 0)
    return pl.pallas_call(
        functools.partial(_outproj_kernel, hd=width // MLSTM_HEADS),
        grid=(B, S // tm),
        in_specs=[
            pl.BlockSpec((None, tm, width), row),
            pl.BlockSpec((None, tm, width), row),
            pl.BlockSpec((None, tm, width), lambda b, i: (b, i, 3)),
            pl.BlockSpec((None, tm, oa.shape[2]), row),
            pl.BlockSpec((None, tm, D), row),
            pl.BlockSpec((1, width), const),
            pl.BlockSpec(w_out.shape, const),
            pl.BlockSpec((1, D), const),
        ],
        out_specs=pl.BlockSpec((None, tm, D), row),
        out_shape=jax.ShapeDtypeStruct((B, S, D), F32),
        compiler_params=_params(2),
        name="out_proj",
    )(hf, hb, proj, oa, x, gm, w_out, gp)


def _mlp_kernel(x_ref, gpre_ref, wup_ref, wdown_ref, gpost_ref, out_ref, h_ref, acc_ref):
    f = pl.program_id(2)

    @pl.when(f == 0)
    def _():
        h_ref[...] = _rms(x_ref[...], gpre_ref[...]).astype(BF16)
        acc_ref[...] = jnp.zeros_like(acc_ref)

    u = jnp.maximum(_dot(h_ref[...], wup_ref[...]), 0.0)
    acc_ref[...] += _dot((u * u).astype(BF16), wdown_ref[...])

    @pl.when(f == pl.num_programs(2) - 1)
    def _():
        out_ref[...] = x_ref[...] + _rms(acc_ref[...], gpost_ref[...])


def _mlp(x, gpre, w_up, w_down, gpost):
    B, S, D = x.shape
    F = w_up.shape[1]
    tm = min(512, S)
    tf = 1024
    row = lambda b, i, f: (b, i, 0)
    const = lambda b, i, f: (0, 0)
    return pl.pallas_call(
        _mlp_kernel,
        grid=(B, S // tm, F // tf),
        in_specs=[
            pl.BlockSpec((None, tm, D), row),
            pl.BlockSpec((1, D), const),
            pl.BlockSpec((D, tf), lambda b, i, f: (0, f)),
            pl.BlockSpec((tf, D), lambda b, i, f: (f, 0)),
            pl.BlockSpec((1, D), const),
        ],
        out_specs=pl.BlockSpec((None, tm, D), row),
        out_shape=jax.ShapeDtypeStruct((B, S, D), F32),
        scratch_shapes=[pltpu.VMEM((tm, D), BF16), pltpu.VMEM((tm, D), F32)],
        compiler_params=_params(3),
        name="mlp",
    )(x, gpre, w_up, w_down, gpost)


def _layer(x, lambda_init, g_pre_mix, w_in, b_gates, mlstm_norm_g, lq1, lk1, lq2, lk2, subln_g,
           w_out, g_post_mix, g_pre_mlp, w_up, w_down, g_post_mlp):
    D = x.shape[-1]
    width = D // 2
    gate0 = 4 * width
    w_main = jnp.concatenate([w_in[:, :gate0], w_in[:, gate0 + N_GATES:]], axis=1).astype(BF16)
    w_g = w_in[:, gate0:gate0 + N_GATES]
    w_gc = jnp.pad(w_g, ((0, 0), (0, LANES - N_GATES))).astype(BF16)
    w_gt = w_g.T.astype(BF16)
    b_c = jnp.pad(b_gates, (0, LANES - N_GATES)).reshape(1, LANES)
    b_t = b_gates.reshape(N_GATES, 1)
    row = lambda v: v.reshape(1, -1)

    proj, lg, lgt = _in_proj(x, row(g_pre_mix), w_main, w_gc, w_gt, b_c, b_t)
    hf, hb = _mlstm(proj, lg, lgt, width)
    oa = _attn(proj, row(lq1), row(lk1), row(lq2), row(lk2), subln_g.reshape(-1, 1), gate0, lambda_init)
    x1 = _out_proj(hf, hb, proj, oa, x, row(mlstm_norm_g), w_out.astype(BF16), row(g_post_mix), width)
    return _mlp(x1, row(g_pre_mlp), w_up.astype(BF16), w_down.astype(BF16), row(g_post_mlp))


def _trunk(x, *weights):
    depth = weights[0].shape[0]
    for l in range(depth):
        lambda_init = 0.8 - 0.6 * math.exp(-0.3 * l)
        x = _layer(x, lambda_init, *[w[l] for w in weights])
    return x


def kernel(x_prompt, x_sample, g_pre_mix, w_in, b_gates, mlstm_norm_g, lambda_q1, lambda_k1, lambda_q2,
           lambda_k2, subln_g, w_out, g_post_mix, g_pre_mlp, w_up, w_down, g_post_mlp):
    weights = (g_pre_mix, w_in, b_gates, mlstm_norm_g, lambda_q1, lambda_k1, lambda_q2, lambda_k2,
               subln_g, w_out, g_post_mix, g_pre_mlp, w_up, w_down, g_post_mlp)
    return (_trunk(x_prompt, *weights), _trunk(x_sample, *weights))
```

```python
import functools
import math
import struct

import jax
import jax.numpy as jnp
from jax import lax
from jax.experimental import pallas as pl
from jax.experimental.pallas import tpu as pltpu

F32 = jnp.float32
BF16 = jnp.bfloat16

EPS = 1e-6
MLSTM_HEADS = 4
DIFF_HEADS = 8
CHUNK = 128
N_GATES = 4 * MLSTM_HEADS
LANES = 128
NEG_BIG = -1e30
VMEM_LIMIT = 56 * 1024 * 1024

ATTN_TQ = 256
ATTN_TK = 512
ATTN_VROWS = 144
N_FEAT = 6


def _f32_bits(x):
    return struct.unpack("<I", struct.pack("<f", x))[0]


def _round_f32(x):
    return struct.unpack("<f", struct.pack("<f", x))[0]


def _round_bf16(x):
    bits = _f32_bits(x)
    bits = (bits + 0x7FFF + ((bits >> 16) & 1)) & 0xFFFF0000
    return struct.unpack("<f", struct.pack("<I", bits))[0]


def _split3(value):
    parts, rest = [], value
    for _ in range(3):
        parts.append(_round_bf16(rest))
        rest = _round_f32(rest - parts[-1])
    assert rest == 0.0
    return parts


LOG2E = _round_f32(math.log2(math.e))
LOG2E_PARTS = _split3(LOG2E)


def _rms(x, g):
    return x * lax.rsqrt(jnp.mean(x * x, axis=-1, keepdims=True) + EPS) * g


def _log_sigmoid(x):
    return jnp.minimum(x, 0.0) - jnp.log1p(jnp.exp(-jnp.abs(x)))


def _dot(a, b):
    return jnp.dot(a, b, preferred_element_type=F32)


def _dot_nt(a, b):
    return lax.dot_general(a, b, (((1,), (1,)), ((), ())), preferred_element_type=F32)


def _params(n_axes):
    return pltpu.CompilerParams(dimension_semantics=("arbitrary",) * n_axes,
                                vmem_limit_bytes=VMEM_LIMIT)


def _inproj_kernel(x_ref, g_ref, w_ref, wgc_ref, wgt_ref, bc_ref, bt_ref,
                   proj_ref, lg_ref, lgt_ref, hn_ref):
    @pl.when(pl.program_id(1) == 0)
    def _():
        hn = _rms(x_ref[...], g_ref[...]).astype(BF16)
        hn_ref[...] = hn
        gc = _dot(hn, wgc_ref[...]) + bc_ref[...]
        col = lax.broadcasted_iota(jnp.int32, gc.shape, 1)
        lg_ref[...] = jnp.where((col & MLSTM_HEADS) != 0, _log_sigmoid(gc), gc)
        gt = _dot_nt(wgt_ref[...], hn) + bt_ref[...]
        row = lax.broadcasted_iota(jnp.int32, gt.shape, 0)
        lgt_ref[...] = jnp.where((row & MLSTM_HEADS) != 0, _log_sigmoid(gt), gt)

    proj_ref[...] = _dot(hn_ref[...], w_ref[...]).astype(BF16)


def _in_proj(x, g, w_main, w_gc, w_gt, b_c, b_t):
    B, S, D = x.shape
    NM = w_main.shape[1]
    tm = min(1024, S)
    tn = 1024
    nst = S // tm
    return pl.pallas_call(
        _inproj_kernel,
        grid=(B * nst, NM // tn),
        in_specs=[
            pl.BlockSpec((None, tm, D), lambda i, j: (i // nst, i % nst, 0)),
            pl.BlockSpec((1, D), lambda i, j: (0, 0)),
            pl.BlockSpec((D, tn), lambda i, j: (0, j)),
            pl.BlockSpec((D, LANES), lambda i, j: (0, 0)),
            pl.BlockSpec((N_GATES, D), lambda i, j: (0, 0)),
            pl.BlockSpec((1, LANES), lambda i, j: (0, 0)),
            pl.BlockSpec((N_GATES, 1), lambda i, j: (0, 0)),
        ],
        out_specs=[
            pl.BlockSpec((None, tm, tn), lambda i, j: (i // nst, i % nst, j)),
            pl.BlockSpec((None, tm, LANES), lambda i, j: (i // nst, i % nst, 0)),
            pl.BlockSpec((None, N_GATES, tm), lambda i, j: (i // nst, 0, i % nst)),
        ],
        out_shape=[
            jax.ShapeDtypeStruct((B, S, NM), BF16),
            jax.ShapeDtypeStruct((B, S, LANES), F32),
            jax.ShapeDtypeStruct((B, N_GATES, S), F32),
        ],
        scratch_shapes=[pltpu.VMEM((tm, D), BF16)],
        compiler_params=_params(2),
        name="in_proj",
    )(x, g, w_main, w_gc, w_gt, b_c, b_t)


def _split2(a):
    hi = a.astype(BF16)
    lo = (a - hi.astype(F32)).astype(BF16)
    return hi, lo


def _mlstm_direction(d, p_ref, lg_ref, lgt_ref, out_ref, c_ref, n_ref, m_ref, hd):
    L = CHUNK
    r_i = lax.broadcasted_iota(jnp.int32, (L, L), 0)
    c_i = lax.broadcasted_iota(jnp.int32, (L, L), 1)
    low, up = c_i <= r_i, c_i >= r_i
    mask = up if d else low
    m_col = mask.astype(BF16)
    m_row = (low if d else up).astype(BF16)
    lg = lg_ref[...]
    lgt = lgt_ref[...]
    lg_hi, lg_lo = _split2(lg)
    a_all = _dot(m_col, lg_hi) + _dot(m_col, lg_lo)
    lgt_hi, lgt_lo = _split2(lgt)
    b_all = _dot(lgt_hi, m_row) + _dot(lgt_lo, m_row)
    inv_sqrt = hd ** -0.5
    for h in range(MLSTM_HEADS):
        idx = d * MLSTM_HEADS + h
        icol = 2 * d * MLSTM_HEADS + h
        fcol = icol + MLSTM_HEADS
        a_col = a_all[:, fcol:fcol + 1]
        r_row = lgt[icol:icol + 1, :] - b_all[fcol:fcol + 1, :]
        m_prev = m_ref[idx][0:1, 0:1]
        dmat = jnp.where(mask, a_col + r_row, NEG_BIG)
        inter = a_col + m_prev
        m_t = jnp.maximum(inter, jnp.max(dmat, axis=1, keepdims=True))
        w_intra = jnp.exp(dmat - m_t)
        w_inter = jnp.exp(inter - m_t) * inv_sqrt

        qb = p_ref[:, h * hd:(h + 1) * hd]
        kb = p_ref[:, (MLSTM_HEADS + h) * hd:(MLSTM_HEADS + h + 1) * hd]
        vb = p_ref[:, (2 * MLSTM_HEADS + h) * hd:(2 * MLSTM_HEADS + h + 1) * hd]
        s = _dot_nt(qb, kb) * (w_intra * inv_sqrt)
        c_old = c_ref[idx]
        n_old = n_ref[idx]
        num = _dot(s.astype(BF16), vb) + w_inter * _dot(qb, c_old.astype(BF16))
        qn = jnp.sum(qb.astype(F32) * n_old[0:1, :], axis=1, keepdims=True)
        den = jnp.sum(s, axis=1, keepdims=True) + w_inter * qn
        denom = jnp.maximum(jnp.abs(den), jnp.exp(-m_t))
        out_ref[:, h * hd:(h + 1) * hd] = num * (1.0 / denom)

        b_tot = jnp.sum(lgt[fcol:fcol + 1, :], axis=1, keepdims=True)
        g_row = b_tot + r_row
        m_new = jnp.maximum(b_tot + m_prev, jnp.max(g_row, axis=1, keepdims=True))
        wk_row = jnp.exp(g_row - m_new)
        decay = jnp.exp(b_tot + m_prev - m_new)
        ktw = (kb.astype(F32).T * wk_row).astype(BF16)
        c_ref[idx] = decay * c_old + _dot(ktw, vb)
        wk8 = jnp.broadcast_to(wk_row, (8, L)).astype(BF16)
        n_ref[idx] = decay * n_old + _dot(wk8, kb)
        m_ref[idx] = jnp.broadcast_to(m_new, (8, LANES))


def _mlstm_kernel(pf_ref, pb_ref, lgf_ref, lgb_ref, lgtf_ref, lgtb_ref, hf_ref, hb_ref,
                  c_ref, n_ref, m_ref, *, hd):
    @pl.when(pl.program_id(1) == 0)
    def _():
        c_ref[...] = jnp.zeros_like(c_ref)
        n_ref[...] = jnp.zeros_like(n_ref)
        m_ref[...] = jnp.zeros_like(m_ref)

    _mlstm_direction(0, pf_ref, lgf_ref, lgtf_ref, hf_ref, c_ref, n_ref, m_ref, hd)
    _mlstm_direction(1, pb_ref, lgb_ref, lgtb_ref, hb_ref, c_ref, n_ref, m_ref, hd)


def _mlstm(proj, lg, lgt, width):
    B, S, _ = proj.shape
    nc = S // CHUNK
    hd = width // MLSTM_HEADS
    fwd = lambda b, c: (b, c, 0)
    bwd = lambda b, c: (b, nc - 1 - c, 0)
    return pl.pallas_call(
        functools.partial(_mlstm_kernel, hd=hd),
        grid=(B, nc),
        in_specs=[
            pl.BlockSpec((None, CHUNK, 3 * width), fwd),
            pl.BlockSpec((None, CHUNK, 3 * width), bwd),
            pl.BlockSpec((None, CHUNK, LANES), fwd),
            pl.BlockSpec((None, CHUNK, LANES), bwd),
            pl.BlockSpec((None, N_GATES, CHUNK), lambda b, c: (b, 0, c)),
            pl.BlockSpec((None, N_GATES, CHUNK), lambda b, c: (b, 0, nc - 1 - c)),
        ],
        out_specs=[
            pl.BlockSpec((None, CHUNK, width), fwd),
            pl.BlockSpec((None, CHUNK, width), bwd),
        ],
        out_shape=[jax.ShapeDtypeStruct((B, S, width), F32)] * 2,
        scratch_shapes=[
            pltpu.VMEM((2 * MLSTM_HEADS, hd, hd), F32),
            pltpu.VMEM((2 * MLSTM_HEADS, 8, hd), F32),
            pltpu.VMEM((2 * MLSTM_HEADS, 8, LANES), F32),
        ],
        compiler_params=_params(2),
        name="mlstm",
    )(proj, proj, lg, lg, lgt, lgt)


def _attn_kernel(q_ref, k_ref, v_ref, lq1_ref, lk1_ref, lq2_ref, lk2_ref, g_ref, o_ref,
                 ka_ref, vt_ref, qat_ref, acc_ref, m_ref, r_ref, sa_ref, sb_ref, bma_ref, bmb_ref,
                 *, seq, tq, tk, lambda_init):
    head = pl.program_id(1)
    qi = pl.program_id(2)
    nk = seq // tk
    hd = LANES // 2
    slope = pltpu.bitcast(jnp.full((8, LANES), (126 - head) << 23, jnp.int32), F32)[0:1, 0:1]
    c_f32 = slope * LOG2E

    @pl.when(qi == 0)
    def _build_keys():
        def body(j, carry):
            start = pl.multiple_of(j * tk, tk)
            pos = start + lax.broadcasted_iota(jnp.int32, (tk, LANES), 0)
            lane = lax.broadcasted_iota(jnp.int32, (tk, LANES), 1)
            s_lo = pos & (LANES - 1)
            feat = jnp.where(lane < 3, s_lo, jnp.where(lane < N_FEAT, pos - s_lo, 0))
            ka_ref[j, :, 0:LANES] = k_ref[pl.ds(start, tk), :]
            ka_ref[j, :, LANES:2 * LANES] = feat.astype(F32).astype(BF16)
            vt_ref[j, 0:LANES, :] = v_ref[pl.ds(start, tk), :].astype(F32).T.astype(BF16)
            ones_row = lax.broadcasted_iota(jnp.int32, (ATTN_VROWS - LANES, tk), 0) == 0
            vt_ref[j, LANES:ATTN_VROWS, :] = jnp.where(ones_row, 1.0, 0.0).astype(BF16)
            return carry
        lax.fori_loop(0, nk, body, 0)

    q0 = qi * tq
    q_t = q_ref[...].astype(F32).T * (hd ** -0.5 * LOG2E)
    row = lax.broadcasted_iota(jnp.int32, (LANES, tq), 0)
    q_maps = (jnp.where(row < hd, q_t, 0.0).astype(BF16), jnp.where(row >= hd, q_t, 0.0).astype(BF16))
    part = jnp.zeros((LANES, tq), F32)
    for r in range(N_FEAT):
        part = jnp.where(row == r, LOG2E_PARTS[r % 3], part)
    feat_q = part * slope
    feats = (feat_q.astype(BF16), (-feat_q).astype(BF16), jnp.zeros((LANES, tq), BF16))
    for mp in range(2):
        for side in range(3):
            qat_ref[3 * mp + side, 0:LANES, :] = q_maps[mp]
            qat_ref[3 * mp + side, LANES:2 * LANES, :] = feats[side]

    t_pos = (q0 + lax.broadcasted_iota(jnp.int32, (1, tq), 1)).astype(F32)
    ct = c_f32 * t_pos
    r_ref[0] = -ct
    r_ref[1] = ct

    m_ref[...] = jnp.full(m_ref.shape, NEG_BIG, F32)
    acc_ref[...] = jnp.zeros_like(acc_ref)
    jd = q0 // tk
    nb = nk - 1

    def block_of(jj):
        j = jj + (jj >= jd).astype(jnp.int32)
        return j, (j > jd).astype(jnp.int32)

    def qk_stage(jj, s_ref, bm_ref):
        j, side = block_of(jj)
        ka = ka_ref[j]
        for mp in range(2):
            s_t = _dot(ka, qat_ref[3 * mp + side])
            s_ref[mp] = s_t
            bm_ref[mp] = jnp.max(s_t, axis=0, keepdims=True) + r_ref[side]

    def softmax_pv(s_ref, bm_ref, r, vt):
        for mp in range(2):
            m_old = m_ref[mp]
            m_new = jnp.maximum(m_old, bm_ref[mp])
            p = jnp.exp2(s_ref[mp] - (m_new - r)).astype(BF16)
            acc_ref[mp] = jnp.exp2(m_old - m_new) * acc_ref[mp] + _dot(vt, p)
            m_ref[mp] = m_new

    def sm_stage(jj, s_ref, bm_ref):
        j, side = block_of(jj)
        softmax_pv(s_ref, bm_ref, r_ref[side], vt_ref[j])

    s_pos = jd * tk + lax.broadcasted_iota(jnp.int32, (tk, tq), 0)
    t_lane = q0 + lax.broadcasted_iota(jnp.int32, (tk, tq), 1)
    dbias = c_f32 * jnp.abs(s_pos - t_lane).astype(F32)
    ka_d = ka_ref[jd]
    for mp in range(2):
        s_t = _dot(ka_d, qat_ref[3 * mp + 2]) - dbias
        sb_ref[mp] = s_t
        bmb_ref[mp] = jnp.max(s_t, axis=0, keepdims=True)
    if nb > 0:
        qk_stage(0, sa_ref, bma_ref)
    softmax_pv(sb_ref, bmb_ref, jnp.zeros((1, tq), F32), vt_ref[jd])

    def pair(i, carry):
        qk_stage(2 * i + 1, sb_ref, bmb_ref)
        sm_stage(2 * i, sa_ref, bma_ref)
        qk_stage(2 * i + 2, sa_ref, bma_ref)
        sm_stage(2 * i + 1, sb_ref, bmb_ref)
        return carry

    if nb > 0:
        n_pairs = (nb - 1) // 2
        lax.fori_loop(0, n_pairs, pair, 0)
        if nb - 1 == 2 * n_pairs:
            sm_stage(nb - 1, sa_ref, bma_ref)
        else:
            qk_stage(nb - 1, sb_ref, bmb_ref)
            sm_stage(nb - 2, sa_ref, bma_ref)
            sm_stage(nb - 1, sb_ref, bmb_ref)

    lam = (jnp.exp(jnp.sum(lq1_ref[...] * lk1_ref[...], axis=1, keepdims=True))
           - jnp.exp(jnp.sum(lq2_ref[...] * lk2_ref[...], axis=1, keepdims=True)) + lambda_init)
    outs = []
    for mp in range(2):
        a = acc_ref[mp]
        outs.append(a[0:LANES, :] * (1.0 / a[LANES:LANES + 1, :]))
    o_t = outs[0] - lam * outs[1]
    ms = jnp.mean(o_t * o_t, axis=0, keepdims=True)
    y = o_t * lax.rsqrt(ms + EPS) * (g_ref[...] * (1.0 - lambda_init))
    o_ref[...] = y.T.astype(BF16)


def _attn(proj, lq1, lk1, lq2, lk2, g_col, col0, lambda_init):
    B, S, _ = proj.shape
    tq, tk = min(ATTN_TQ, S), min(ATTN_TK, S)
    nk = S // tk
    cb = col0 // LANES
    vec = pl.BlockSpec((1, LANES // 2), lambda b, h, i: (0, 0))
    return pl.pallas_call(
        functools.partial(_attn_kernel, seq=S, tq=tq, tk=tk, lambda_init=lambda_init),
        grid=(B, DIFF_HEADS, S // tq),
        in_specs=[
            pl.BlockSpec((None, tq, LANES), lambda b, h, i: (b, i, cb + h)),
            pl.BlockSpec((None, S, LANES), lambda b, h, i: (b, 0, cb + DIFF_HEADS + h)),
            pl.BlockSpec((None, S, LANES), lambda b, h, i: (b, 0, cb + 2 * DIFF_HEADS + h)),
            vec, vec, vec, vec,
            pl.BlockSpec((LANES, 1), lambda b, h, i: (0, 0)),
        ],
        out_specs=pl.BlockSpec((None, tq, LANES), lambda b, h, i: (b, i, h)),
        out_shape=jax.ShapeDtypeStruct((B, S, DIFF_HEADS * LANES), BF16),
        scratch_shapes=[
            pltpu.VMEM((nk, tk, 2 * LANES), BF16),
            pltpu.VMEM((nk, ATTN_VROWS, tk), BF16),
            pltpu.VMEM((6, 2 * LANES, tq), BF16),
            pltpu.VMEM((2, ATTN_VROWS, tq), F32),
            pltpu.VMEM((2, 1, tq), F32),
            pltpu.VMEM((2, 1, tq), F32),
            pltpu.VMEM((2, tk, tq), F32),
            pltpu.VMEM((2, tk, tq), F32),
            pltpu.VMEM((2, 1, tq), F32),
            pltpu.VMEM((2, 1, tq), F32),
        ],
        compiler_params=_params(3),
        name="attn",
    )(proj, proj, proj, lq1, lk1, lq2, lk2, g_col)


def _outproj_kernel(hf_ref, hb_ref, mo_ref, oa_ref, x_ref, gm_ref, w_ref, gp_ref, out_ref, *, hd):
    width = MLSTM_HEADS * hd
    ys = []
    for h in range(MLSTM_HEADS):
        sl = slice(h * hd, (h + 1) * hd)
        hm = _rms(hf_ref[:, sl] + hb_ref[:, sl], gm_ref[:, sl])
        ys.append((jax.nn.sigmoid(mo_ref[:, sl].astype(F32)) * hm).astype(BF16))
    ym = jnp.concatenate(ys, axis=1)
    mix = _dot(ym, w_ref[0:width, :]) + _dot(oa_ref[...], w_ref[width:, :])
    out_ref[...] = x_ref[...] + _rms(mix, gp_ref[...])


def _out_proj(hf, hb, proj, oa, x, gm, w_out, gp, width):
    B, S, D = x.shape
    tm = min(256, S)
    row = lambda b, i: (b, i, 0)
    const = lambda b, i: (0, 0)
    return pl.pallas_call(
        functools.partial(_outproj_kernel, hd=width // MLSTM_HEADS),
        grid=(B, S // tm),
        in_specs=[
            pl.BlockSpec((None, tm, width), row),
            pl.BlockSpec((None, tm, width), row),
            pl.BlockSpec((None, tm, width), lambda b, i: (b, i, 3)),
            pl.BlockSpec((None, tm, oa.shape[2]), row),
            pl.BlockSpec((None, tm, D), row),
            pl.BlockSpec((1, width), const),
            pl.BlockSpec(w_out.shape, const),
            pl.BlockSpec((1, D), const),
        ],
        out_specs=pl.BlockSpec((None, tm, D), row),
        out_shape=jax.ShapeDtypeStruct((B, S, D), F32),
        compiler_params=_params(2),
        name="out_proj",
    )(hf, hb, proj, oa, x, gm, w_out, gp)


def _mlp_kernel(x_ref, gpre_ref, wup_ref, wdown_ref, gpost_ref, out_ref, h_ref, acc_ref):
    f = pl.program_id(2)

    @pl.when(f == 0)
    def _():
        h_ref[...] = _rms(x_ref[...], gpre_ref[...]).astype(BF16)
        acc_ref[...] = jnp.zeros_like(acc_ref)

    u = jnp.maximum(_dot(h_ref[...], wup_ref[...]), 0.0)
    acc_ref[...] += _dot((u * u).astype(BF16), wdown_ref[...])

    @pl.when(f == pl.num_programs(2) - 1)
    def _():
        out_ref[...] = x_ref[...] + _rms(acc_ref[...], gpost_ref[...])


def _mlp(x, gpre, w_up, w_down, gpost):
    B, S, D = x.shape
    F = w_up.shape[1]
    tm = min(512, S)
    tf = 1024
    row = lambda b, i, f: (b, i, 0)
    const = lambda b, i, f: (0, 0)
    return pl.pallas_call(
        _mlp_kernel,
        grid=(B, S // tm, F // tf),
        in_specs=[
            pl.BlockSpec((None, tm, D), row),
            pl.BlockSpec((1, D), const),
            pl.BlockSpec((D, tf), lambda b, i, f: (0, f)),
            pl.BlockSpec((tf, D), lambda b, i, f: (f, 0)),
            pl.BlockSpec((1, D), const),
        ],
        out_specs=pl.BlockSpec((None, tm, D), row),
        out_shape=jax.ShapeDtypeStruct((B, S, D), F32),
        scratch_shapes=[pltpu.VMEM((tm, D), BF16), pltpu.VMEM((tm, D), F32)],
        compiler_params=_params(3),
        name="mlp",
    )(x, gpre, w_up, w_down, gpost)


def _layer(x, lambda_init, g_pre_mix, w_in, b_gates, mlstm_norm_g, lq1, lk1, lq2, lk2, subln_g,
           w_out, g_post_mix, g_pre_mlp, w_up, w_down, g_post_mlp):
    D = x.shape[-1]
    width = D // 2
    gate0 = 4 * width
    w_main = jnp.concatenate([w_in[:, :gate0], w_in[:, gate0 + N_GATES:]], axis=1).astype(BF16)
    w_g = w_in[:, gate0:gate0 + N_GATES]
    w_gc = jnp.pad(w_g, ((0, 0), (0, LANES - N_GATES))).astype(BF16)
    w_gt = w_g.T.astype(BF16)
    b_c = jnp.pad(b_gates, (0, LANES - N_GATES)).reshape(1, LANES)
    b_t = b_gates.reshape(N_GATES, 1)
    row = lambda v: v.reshape(1, -1)

    proj, lg, lgt = _in_proj(x, row(g_pre_mix), w_main, w_gc, w_gt, b_c, b_t)
    hf, hb = _mlstm(proj, lg, lgt, width)
    oa = _attn(proj, row(lq1), row(lk1), row(lq2), row(lk2), subln_g.reshape(-1, 1), gate0, lambda_init)
    x1 = _out_proj(hf, hb, proj, oa, x, row(mlstm_norm_g), w_out.astype(BF16), row(g_post_mix), width)
    return _mlp(x1, row(g_pre_mlp), w_up.astype(BF16), w_down.astype(BF16), row(g_post_mlp))


def _trunk(x, *weights):
    depth = weights[0].shape[0]
    for l in range(depth):
        lambda_init = 0.8 - 0.6 * math.exp(-0.3 * l)
        x = _layer(x, lambda_init, *[w[l] for w in weights])
    return x


def kernel(x_prompt, x_sample, g_pre_mix, w_in, b_gates, mlstm_norm_g, lambda_q1, lambda_k1, lambda_q2,
           lambda_k2, subln_g, w_out, g_post_mix, g_pre_mlp, w_up, w_down, g_post_mlp):
    weights = (g_pre_mix, w_in, b_gates, mlstm_norm_g, lambda_q1, lambda_k1, lambda_q2, lambda_k2,
               subln_g, w_out, g_post_mix, g_pre_mlp, w_up, w_down, g_post_mlp)
    return (_trunk(x_prompt, *weights), _trunk(x_sample, *weights))
```

```python
import functools
import math
import struct

import jax
import jax.numpy as jnp
from jax import lax
from jax.experimental import pallas as pl
from jax.experimental.pallas import tpu as pltpu

F32 = jnp.float32
BF16 = jnp.bfloat16

EPS = 1e-6
MLSTM_HEADS = 4
DIFF_HEADS = 8
CHUNK = 128
N_GATES = 4 * MLSTM_HEADS
LANES = 128
NEG_BIG = -1e30
VMEM_LIMIT = 56 * 1024 * 1024

ATTN_TQ = 256
ATTN_TK = 256
ATTN_VROWS = 144
N_FEAT = 6


def _f32_bits(x):
    return struct.unpack("<I", struct.pack("<f", x))[0]


def _round_f32(x):
    return struct.unpack("<f", struct.pack("<f", x))[0]


def _round_bf16(x):
    bits = _f32_bits(x)
    bits = (bits + 0x7FFF + ((bits >> 16) & 1)) & 0xFFFF0000
    return struct.unpack("<f", struct.pack("<I", bits))[0]


def _split3(value):
    parts, rest = [], value
    for _ in range(3):
        parts.append(_round_bf16(rest))
        rest = _round_f32(rest - parts[-1])
    assert rest == 0.0
    return parts


LOG2E = _round_f32(math.log2(math.e))
LOG2E_PARTS = _split3(LOG2E)


def _rms(x, g):
    return x * lax.rsqrt(jnp.mean(x * x, axis=-1, keepdims=True) + EPS) * g


def _log_sigmoid(x):
    return jnp.minimum(x, 0.0) - jnp.log1p(jnp.exp(-jnp.abs(x)))


def _dot(a, b):
    return jnp.dot(a, b, preferred_element_type=F32)


def _dot_nt(a, b):
    return lax.dot_general(a, b, (((1,), (1,)), ((), ())), preferred_element_type=F32)


def _params(n_axes, flags=None):
    return pltpu.CompilerParams(dimension_semantics=("arbitrary",) * n_axes,
                                vmem_limit_bytes=VMEM_LIMIT, flags=flags)


def _inproj_kernel(x_ref, g_ref, w_ref, wgc_ref, wgt_ref, bc_ref, bt_ref,
                   proj_ref, lg_ref, lgt_ref, hn_ref):
    @pl.when(pl.program_id(1) == 0)
    def _():
        hn = _rms(x_ref[...], g_ref[...]).astype(BF16)
        hn_ref[...] = hn
        gc = _dot(hn, wgc_ref[...]) + bc_ref[...]
        col = lax.broadcasted_iota(jnp.int32, gc.shape, 1)
        lg_ref[...] = jnp.where((col & MLSTM_HEADS) != 0, _log_sigmoid(gc), gc)
        gt = _dot_nt(wgt_ref[...], hn) + bt_ref[...]
        row = lax.broadcasted_iota(jnp.int32, gt.shape, 0)
        lgt_ref[...] = jnp.where((row & MLSTM_HEADS) != 0, _log_sigmoid(gt), gt)

    proj_ref[...] = _dot(hn_ref[...], w_ref[...]).astype(BF16)


def _in_proj(x, g, w_main, w_gc, w_gt, b_c, b_t):
    B, S, D = x.shape
    NM = w_main.shape[1]
    tm = min(1024, S)
    tn = 1024
    nst = S // tm
    return pl.pallas_call(
        _inproj_kernel,
        grid=(B * nst, NM // tn),
        in_specs=[
            pl.BlockSpec((None, tm, D), lambda i, j: (i // nst, i % nst, 0)),
            pl.BlockSpec((1, D), lambda i, j: (0, 0)),
            pl.BlockSpec((D, tn), lambda i, j: (0, j)),
            pl.BlockSpec((D, LANES), lambda i, j: (0, 0)),
            pl.BlockSpec((N_GATES, D), lambda i, j: (0, 0)),
            pl.BlockSpec((1, LANES), lambda i, j: (0, 0)),
            pl.BlockSpec((N_GATES, 1), lambda i, j: (0, 0)),
        ],
        out_specs=[
            pl.BlockSpec((None, tm, tn), lambda i, j: (i // nst, i % nst, j)),
            pl.BlockSpec((None, tm, LANES), lambda i, j: (i // nst, i % nst, 0)),
            pl.BlockSpec((None, N_GATES, tm), lambda i, j: (i // nst, 0, i % nst)),
        ],
        out_shape=[
            jax.ShapeDtypeStruct((B, S, NM), BF16),
            jax.ShapeDtypeStruct((B, S, LANES), F32),
            jax.ShapeDtypeStruct((B, N_GATES, S), F32),
        ],
        scratch_shapes=[pltpu.VMEM((tm, D), BF16)],
        compiler_params=_params(2),
        name="in_proj",
    )(x, g, w_main, w_gc, w_gt, b_c, b_t)


def _split2(a):
    hi = a.astype(BF16)
    lo = (a - hi.astype(F32)).astype(BF16)
    return hi, lo


def _mlstm_direction(d, p_ref, lg_ref, lgt_ref, out_ref, c_ref, n_ref, m_ref, hd):
    L = CHUNK
    r_i = lax.broadcasted_iota(jnp.int32, (L, L), 0)
    c_i = lax.broadcasted_iota(jnp.int32, (L, L), 1)
    low, up = c_i <= r_i, c_i >= r_i
    mask = up if d else low
    m_col = mask.astype(BF16)
    m_row = (low if d else up).astype(BF16)
    lg = lg_ref[...]
    lgt = lgt_ref[...]
    lg_hi, lg_lo = _split2(lg)
    a_all = _dot(m_col, lg_hi) + _dot(m_col, lg_lo)
    lgt_hi, lgt_lo = _split2(lgt)
    b_all = _dot(lgt_hi, m_row) + _dot(lgt_lo, m_row)
    inv_sqrt = hd ** -0.5
    for h in range(MLSTM_HEADS):
        idx = d * MLSTM_HEADS + h
        icol = 2 * d * MLSTM_HEADS + h
        fcol = icol + MLSTM_HEADS
        a_col = a_all[:, fcol:fcol + 1]
        r_row = lgt[icol:icol + 1, :] - b_all[fcol:fcol + 1, :]
        m_prev = m_ref[idx][0:1, 0:1]
        dmat = jnp.where(mask, a_col + r_row, NEG_BIG)
        inter = a_col + m_prev
        m_t = jnp.maximum(inter, jnp.max(dmat, axis=1, keepdims=True))
        w_intra = jnp.exp(dmat - m_t)
        w_inter = jnp.exp(inter - m_t) * inv_sqrt

        qb = p_ref[:, h * hd:(h + 1) * hd]
        kb = p_ref[:, (MLSTM_HEADS + h) * hd:(MLSTM_HEADS + h + 1) * hd]
        vb = p_ref[:, (2 * MLSTM_HEADS + h) * hd:(2 * MLSTM_HEADS + h + 1) * hd]
        s = _dot_nt(qb, kb) * (w_intra * inv_sqrt)
        c_old = c_ref[idx]
        n_old = n_ref[idx]
        num = _dot(s.astype(BF16), vb) + w_inter * _dot(qb, c_old.astype(BF16))
        qn = jnp.sum(qb.astype(F32) * n_old[0:1, :], axis=1, keepdims=True)
        den = jnp.sum(s, axis=1, keepdims=True) + w_inter * qn
        denom = jnp.maximum(jnp.abs(den), jnp.exp(-m_t))
        out_ref[:, h * hd:(h + 1) * hd] = num * (1.0 / denom)

        b_tot = jnp.sum(lgt[fcol:fcol + 1, :], axis=1, keepdims=True)
        g_row = b_tot + r_row
        m_new = jnp.maximum(b_tot + m_prev, jnp.max(g_row, axis=1, keepdims=True))
        wk_row = jnp.exp(g_row - m_new)
        decay = jnp.exp(b_tot + m_prev - m_new)
        ktw = (kb.astype(F32).T * wk_row).astype(BF16)
        c_ref[idx] = decay * c_old + _dot(ktw, vb)
        wk8 = jnp.broadcast_to(wk_row, (8, L)).astype(BF16)
        n_ref[idx] = decay * n_old + _dot(wk8, kb)
        m_ref[idx] = jnp.broadcast_to(m_new, (8, LANES))


def _mlstm_kernel(pf_ref, pb_ref, lgf_ref, lgb_ref, lgtf_ref, lgtb_ref, hf_ref, hb_ref,
                  c_ref, n_ref, m_ref, *, hd):
    @pl.when(pl.program_id(1) == 0)
    def _():
        c_ref[...] = jnp.zeros_like(c_ref)
        n_ref[...] = jnp.zeros_like(n_ref)
        m_ref[...] = jnp.zeros_like(m_ref)

    _mlstm_direction(0, pf_ref, lgf_ref, lgtf_ref, hf_ref, c_ref, n_ref, m_ref, hd)
    _mlstm_direction(1, pb_ref, lgb_ref, lgtb_ref, hb_ref, c_ref, n_ref, m_ref, hd)


def _mlstm(proj, lg, lgt, width):
    B, S, _ = proj.shape
    nc = S // CHUNK
    hd = width // MLSTM_HEADS
    fwd = lambda b, c: (b, c, 0)
    bwd = lambda b, c: (b, nc - 1 - c, 0)
    return pl.pallas_call(
        functools.partial(_mlstm_kernel, hd=hd),
        grid=(B, nc),
        in_specs=[
            pl.BlockSpec((None, CHUNK, 3 * width), fwd),
            pl.BlockSpec((None, CHUNK, 3 * width), bwd),
            pl.BlockSpec((None, CHUNK, LANES), fwd),
            pl.BlockSpec((None, CHUNK, LANES), bwd),
            pl.BlockSpec((None, N_GATES, CHUNK), lambda b, c: (b, 0, c)),
            pl.BlockSpec((None, N_GATES, CHUNK), lambda b, c: (b, 0, nc - 1 - c)),
        ],
        out_specs=[
            pl.BlockSpec((None, CHUNK, width), fwd),
            pl.BlockSpec((None, CHUNK, width), bwd),
        ],
        out_shape=[jax.ShapeDtypeStruct((B, S, width), F32)] * 2,
        scratch_shapes=[
            pltpu.VMEM((2 * MLSTM_HEADS, hd, hd), F32),
            pltpu.VMEM((2 * MLSTM_HEADS, 8, hd), F32),
            pltpu.VMEM((2 * MLSTM_HEADS, 8, LANES), F32),
        ],
        compiler_params=_params(2),
        name="mlstm",
    )(proj, proj, lg, lg, lgt, lgt)


def _attn_kernel(q_ref, k_ref, v_ref, lq1_ref, lk1_ref, lq2_ref, lk2_ref, g_ref, o_ref,
                 ka_ref, vt_ref, qat_ref, acc_ref, m_ref, r_ref, *bufs, seq, tq, tk, lambda_init):
    s_refs, bm_refs, p_refs, al_refs = bufs[0:3], bufs[3:6], bufs[6:8], bufs[8:10]
    head = pl.program_id(1)
    qi = pl.program_id(2)
    nk = seq // tk
    hd = LANES // 2
    slope = pltpu.bitcast(jnp.full((8, LANES), (126 - head) << 23, jnp.int32), F32)[0:1, 0:1]
    c_f32 = slope * LOG2E

    @pl.when(qi == 0)
    def _build_keys():
        def body(j, carry):
            start = pl.multiple_of(j * tk, tk)
            pos = start + lax.broadcasted_iota(jnp.int32, (tk, LANES), 0)
            lane = lax.broadcasted_iota(jnp.int32, (tk, LANES), 1)
            s_lo = pos & (LANES - 1)
            feat = jnp.where(lane < 3, s_lo, jnp.where(lane < N_FEAT, pos - s_lo, 0))
            ka_ref[j, :, 0:LANES] = k_ref[pl.ds(start, tk), :]
            ka_ref[j, :, LANES:2 * LANES] = feat.astype(F32).astype(BF16)
            vt_ref[j, 0:LANES, :] = v_ref[pl.ds(start, tk), :].astype(F32).T.astype(BF16)
            ones_row = lax.broadcasted_iota(jnp.int32, (ATTN_VROWS - LANES, tk), 0) == 0
            vt_ref[j, LANES:ATTN_VROWS, :] = jnp.where(ones_row, 1.0, 0.0).astype(BF16)
            return carry
        lax.fori_loop(0, nk, body, 0)

    q0 = qi * tq
    q_t = q_ref[...].astype(F32).T * (hd ** -0.5 * LOG2E)
    row = lax.broadcasted_iota(jnp.int32, (LANES, tq), 0)
    q_maps = (jnp.where(row < hd, q_t, 0.0).astype(BF16), jnp.where(row >= hd, q_t, 0.0).astype(BF16))
    part = jnp.zeros((LANES, tq), F32)
    for r in range(N_FEAT):
        part = jnp.where(row == r, LOG2E_PARTS[r % 3], part)
    feat_q = part * slope
    feats = (feat_q.astype(BF16), (-feat_q).astype(BF16), jnp.zeros((LANES, tq), BF16))
    for mp in range(2):
        for side in range(3):
            qat_ref[3 * mp + side, 0:LANES, :] = q_maps[mp]
            qat_ref[3 * mp + side, LANES:2 * LANES, :] = feats[side]

    t_pos = (q0 + lax.broadcasted_iota(jnp.int32, (1, tq), 1)).astype(F32)
    ct = c_f32 * t_pos
    r_ref[0] = -ct
    r_ref[1] = ct
    r_ref[2] = jnp.zeros_like(ct)

    m_ref[...] = jnp.full(m_ref.shape, NEG_BIG, F32)
    acc_ref[...] = jnp.zeros_like(acc_ref)
    jd = q0 // tk
    nb = nk - 1

    def block_of(jj):
        j = jj + (jj >= jd).astype(jnp.int32)
        return j, (j > jd).astype(jnp.int32)

    def block(idx):
        return (jd, 2) if idx == 0 else block_of(idx - 1)

    def qk_stage(idx):
        j, side = block(idx)
        s_ref, bm_ref = s_refs[idx % 3], bm_refs[idx % 3]
        ka = ka_ref[j]
        for mp in range(2):
            s_t = _dot(ka, qat_ref[3 * mp + side])
            if idx == 0:
                s_pos = jd * tk + lax.broadcasted_iota(jnp.int32, (tk, tq), 0)
                t_lane = q0 + lax.broadcasted_iota(jnp.int32, (tk, tq), 1)
                s_t = s_t - c_f32 * jnp.abs(s_pos - t_lane).astype(F32)
            s_ref[mp] = s_t
            bm_ref[mp] = jnp.max(s_t, axis=0, keepdims=True) + r_ref[side]

    def exp_stage(idx):
        _, side = block(idx)
        s_ref, bm_ref = s_refs[idx % 3], bm_refs[idx % 3]
        p_ref, al_ref = p_refs[idx % 2], al_refs[idx % 2]
        r = r_ref[side]
        for mp in range(2):
            m_old = m_ref[mp]
            m_new = jnp.maximum(m_old, bm_ref[mp])
            p_ref[mp] = jnp.exp2(s_ref[mp] - (m_new - r)).astype(BF16)
            al_ref[mp] = jnp.exp2(m_old - m_new)
            m_ref[mp] = m_new

    def pv_stage(idx):
        j, _ = block(idx)
        p_ref, al_ref = p_refs[idx % 2], al_refs[idx % 2]
        vt = vt_ref[j]
        for mp in range(2):
            acc_ref[mp] = al_ref[mp] * acc_ref[mp] + _dot(vt, p_ref[mp])

    for step in range(nk + 2):
        if step < nk:
            qk_stage(step)
        if 0 <= step - 1 < nk:
            exp_stage(step - 1)
        if 0 <= step - 2 < nk:
            pv_stage(step - 2)

    lam = (jnp.exp(jnp.sum(lq1_ref[...] * lk1_ref[...], axis=1, keepdims=True))
           - jnp.exp(jnp.sum(lq2_ref[...] * lk2_ref[...], axis=1, keepdims=True)) + lambda_init)
    outs = []
    for mp in range(2):
        a = acc_ref[mp]
        outs.append(a[0:LANES, :] * (1.0 / a[LANES:LANES + 1, :]))
    o_t = outs[0] - lam * outs[1]
    ms = jnp.mean(o_t * o_t, axis=0, keepdims=True)
    y = o_t * lax.rsqrt(ms + EPS) * (g_ref[...] * (1.0 - lambda_init))
    o_ref[...] = y.T.astype(BF16)


def _attn(proj, lq1, lk1, lq2, lk2, g_col, col0, lambda_init):
    B, S, _ = proj.shape
    tq, tk = min(ATTN_TQ, S), min(ATTN_TK, S)
    nk = S // tk
    cb = col0 // LANES
    vec = pl.BlockSpec((1, LANES // 2), lambda b, h, i: (0, 0))
    return pl.pallas_call(
        functools.partial(_attn_kernel, seq=S, tq=tq, tk=tk, lambda_init=lambda_init),
        grid=(B, DIFF_HEADS, S // tq),
        in_specs=[
            pl.BlockSpec((None, tq, LANES), lambda b, h, i: (b, i, cb + h)),
            pl.BlockSpec((None, S, LANES), lambda b, h, i: (b, 0, cb + DIFF_HEADS + h)),
            pl.BlockSpec((None, S, LANES), lambda b, h, i: (b, 0, cb + 2 * DIFF_HEADS + h)),
            vec, vec, vec, vec,
            pl.BlockSpec((LANES, 1), lambda b, h, i: (0, 0)),
        ],
        out_specs=pl.BlockSpec((None, tq, LANES), lambda b, h, i: (b, i, h)),
        out_shape=jax.ShapeDtypeStruct((B, S, DIFF_HEADS * LANES), BF16),
        scratch_shapes=[
            pltpu.VMEM((nk, tk, 2 * LANES), BF16),
            pltpu.VMEM((nk, ATTN_VROWS, tk), BF16),
            pltpu.VMEM((6, 2 * LANES, tq), BF16),
            pltpu.VMEM((2, ATTN_VROWS, tq), F32),
            pltpu.VMEM((2, 1, tq), F32),
            pltpu.VMEM((3, 1, tq), F32),
        ] + [pltpu.VMEM((2, tk, tq), F32)] * 3 + [pltpu.VMEM((2, 1, tq), F32)] * 3
          + [pltpu.VMEM((2, tk, tq), BF16)] * 2 + [pltpu.VMEM((2, 1, tq), F32)] * 2,
        compiler_params=_params(3),
        name="attn",
    )(proj, proj, proj, lq1, lk1, lq2, lk2, g_col)


def _outproj_kernel(hf_ref, hb_ref, mo_ref, oa_ref, x_ref, gm_ref, w_ref, gp_ref, out_ref, *, hd):
    width = MLSTM_HEADS * hd
    ys = []
    for h in range(MLSTM_HEADS):
        sl = slice(h * hd, (h + 1) * hd)
        hm = _rms(hf_ref[:, sl] + hb_ref[:, sl], gm_ref[:, sl])
        ys.append((jax.nn.sigmoid(mo_ref[:, sl].astype(F32)) * hm).astype(BF16))
    ym = jnp.concatenate(ys, axis=1)
    mix = _dot(ym, w_ref[0:width, :]) + _dot(oa_ref[...], w_ref[width:, :])
    out_ref[...] = x_ref[...] + _rms(mix, gp_ref[...])


def _out_proj(hf, hb, proj, oa, x, gm, w_out, gp, width):
    B, S, D = x.shape
    tm = min(256, S)
    row = lambda b, i: (b, i, 0)
    const = lambda b, i: (0, 0)
    return pl.pallas_call(
        functools.partial(_outproj_kernel, hd=width // MLSTM_HEADS),
        grid=(B, S // tm),
        in_specs=[
            pl.BlockSpec((None, tm, width), row),
            pl.BlockSpec((None, tm, width), row),
            pl.BlockSpec((None, tm, width), lambda b, i: (b, i, 3)),
            pl.BlockSpec((None, tm, oa.shape[2]), row),
            pl.BlockSpec((None, tm, D), row),
            pl.BlockSpec((1, width), const),
            pl.BlockSpec(w_out.shape, const),
            pl.BlockSpec((1, D), const),
        ],
        out_specs=pl.BlockSpec((None, tm, D), row),
        out_shape=jax.ShapeDtypeStruct((B, S, D), F32),
        compiler_params=_params(2),
        name="out_proj",
    )(hf, hb, proj, oa, x, gm, w_out, gp)


def _mlp_kernel(x_ref, gpre_ref, wup_ref, wdown_ref, gpost_ref, out_ref, h_ref, acc_ref):
    f = pl.program_id(2)

    @pl.when(f == 0)
    def _():
        h_ref[...] = _rms(x_ref[...], gpre_ref[...]).astype(BF16)
        acc_ref[...] = jnp.zeros_like(acc_ref)

    u = jnp.maximum(_dot(h_ref[...], wup_ref[...]), 0.0)
    acc_ref[...] += _dot((u * u).astype(BF16), wdown_ref[...])

    @pl.when(f == pl.num_programs(2) - 1)
    def _():
        out_ref[...] = x_ref[...] + _rms(acc_ref[...], gpost_ref[...])


def _mlp(x, gpre, w_up, w_down, gpost):
    B, S, D = x.shape
    F = w_up.shape[1]
    tm = min(512, S)
    tf = 1024
    row = lambda b, i, f: (b, i, 0)
    const = lambda b, i, f: (0, 0)
    return pl.pallas_call(
        _mlp_kernel,
        grid=(B, S // tm, F // tf),
        in_specs=[
            pl.BlockSpec((None, tm, D), row),
            pl.BlockSpec((1, D), const),
            pl.BlockSpec((D, tf), lambda b, i, f: (0, f)),
            pl.BlockSpec((tf, D), lambda b, i, f: (f, 0)),
            pl.BlockSpec((1, D), const),
        ],
        out_specs=pl.BlockSpec((None, tm, D), row),
        out_shape=jax.ShapeDtypeStruct((B, S, D), F32),
        scratch_shapes=[pltpu.VMEM((tm, D), BF16), pltpu.VMEM((tm, D), F32)],
        compiler_params=_params(3),
        name="mlp",
    )(x, gpre, w_up, w_down, gpost)


def _layer(x, lambda_init, g_pre_mix, w_in, b_gates, mlstm_norm_g, lq1, lk1, lq2, lk2, subln_g,
           w_out, g_post_mix, g_pre_mlp, w_up, w_down, g_post_mlp):
    D = x.shape[-1]
    width = D // 2
    gate0 = 4 * width
    w_main = jnp.concatenate([w_in[:, :gate0], w_in[:, gate0 + N_GATES:]], axis=1).astype(BF16)
    w_g = w_in[:, gate0:gate0 + N_GATES]
    w_gc = jnp.pad(w_g, ((0, 0), (0, LANES - N_GATES))).astype(BF16)
    w_gt = w_g.T.astype(BF16)
    b_c = jnp.pad(b_gates, (0, LANES - N_GATES)).reshape(1, LANES)
    b_t = b_gates.reshape(N_GATES, 1)
    row = lambda v: v.reshape(1, -1)

    proj, lg, lgt = _in_proj(x, row(g_pre_mix), w_main, w_gc, w_gt, b_c, b_t)
    hf, hb = _mlstm(proj, lg, lgt, width)
    oa = _attn(proj, row(lq1), row(lk1), row(lq2), row(lk2), subln_g.reshape(-1, 1), gate0, lambda_init)
    x1 = _out_proj(hf, hb, proj, oa, x, row(mlstm_norm_g), w_out.astype(BF16), row(g_post_mix), width)
    return _mlp(x1, row(g_pre_mlp), w_up.astype(BF16), w_down.astype(BF16), row(g_post_mlp))


def _trunk(x, *weights):
    depth = weights[0].shape[0]
    for l in range(depth):
        lambda_init = 0.8 - 0.6 * math.exp(-0.3 * l)
        x = _layer(x, lambda_init, *[w[l] for w in weights])
    return x


def kernel(x_prompt, x_sample, g_pre_mix, w_in, b_gates, mlstm_norm_g, lambda_q1, lambda_k1, lambda_q2,
           lambda_k2, subln_g, w_out, g_post_mix, g_pre_mlp, w_up, w_down, g_post_mlp):
    weights = (g_pre_mix, w_in, b_gates, mlstm_norm_g, lambda_q1, lambda_k1, lambda_q2, lambda_k2,
               subln_g, w_out, g_post_mix, g_pre_mlp, w_up, w_down, g_post_mlp)
    return (_trunk(x_prompt, *weights), _trunk(x_sample, *weights))
```

```python
import functools
import math
import struct

import jax
import jax.numpy as jnp
from jax import lax
from jax.experimental import pallas as pl
from jax.experimental.pallas import tpu as pltpu

F32 = jnp.float32
BF16 = jnp.bfloat16

EPS = 1e-6
MLSTM_HEADS = 4
DIFF_HEADS = 8
CHUNK = 128
N_GATES = 4 * MLSTM_HEADS
LANES = 128
NEG_BIG = -1e30
VMEM_LIMIT = 56 * 1024 * 1024

ATTN_TQ = 256
ATTN_TK = 256
ATTN_VROWS = 144
N_FEAT = 6
ATTN_LOOKAHEAD = 4
ATTN_MAX_EXCESS = 64.0


def _f32_bits(x):
    return struct.unpack("<I", struct.pack("<f", x))[0]


def _round_f32(x):
    return struct.unpack("<f", struct.pack("<f", x))[0]


def _round_bf16(x):
    bits = _f32_bits(x)
    bits = (bits + 0x7FFF + ((bits >> 16) & 1)) & 0xFFFF0000
    return struct.unpack("<f", struct.pack("<I", bits))[0]


def _split3(value):
    parts, rest = [], value
    for _ in range(3):
        parts.append(_round_bf16(rest))
        rest = _round_f32(rest - parts[-1])
    assert rest == 0.0
    return parts


LOG2E = _round_f32(math.log2(math.e))
LOG2E_PARTS = _split3(LOG2E)


def _rms(x, g):
    return x * lax.rsqrt(jnp.mean(x * x, axis=-1, keepdims=True) + EPS) * g


def _log_sigmoid(x):
    return jnp.minimum(x, 0.0) - jnp.log1p(jnp.exp(-jnp.abs(x)))


def _dot(a, b):
    return jnp.dot(a, b, preferred_element_type=F32)


def _dot_nt(a, b):
    return lax.dot_general(a, b, (((1,), (1,)), ((), ())), preferred_element_type=F32)


def _params(n_axes, flags=None):
    return pltpu.CompilerParams(dimension_semantics=("arbitrary",) * n_axes,
                                vmem_limit_bytes=VMEM_LIMIT, flags=flags)


def _inproj_kernel(x_ref, g_ref, w_ref, wgc_ref, wgt_ref, bc_ref, bt_ref,
                   proj_ref, lg_ref, lgt_ref, hn_ref):
    @pl.when(pl.program_id(1) == 0)
    def _():
        hn = _rms(x_ref[...], g_ref[...]).astype(BF16)
        hn_ref[...] = hn
        gc = _dot(hn, wgc_ref[...]) + bc_ref[...]
        col = lax.broadcasted_iota(jnp.int32, gc.shape, 1)
        lg_ref[...] = jnp.where((col & MLSTM_HEADS) != 0, _log_sigmoid(gc), gc)
        gt = _dot_nt(wgt_ref[...], hn) + bt_ref[...]
        row = lax.broadcasted_iota(jnp.int32, gt.shape, 0)
        lgt_ref[...] = jnp.where((row & MLSTM_HEADS) != 0, _log_sigmoid(gt), gt)

    proj_ref[...] = _dot(hn_ref[...], w_ref[...]).astype(BF16)


def _in_proj(x, g, w_main, w_gc, w_gt, b_c, b_t):
    B, S, D = x.shape
    NM = w_main.shape[1]
    tm = min(1024, S)
    tn = 1024
    nst = S // tm
    return pl.pallas_call(
        _inproj_kernel,
        grid=(B * nst, NM // tn),
        in_specs=[
            pl.BlockSpec((None, tm, D), lambda i, j: (i // nst, i % nst, 0)),
            pl.BlockSpec((1, D), lambda i, j: (0, 0)),
            pl.BlockSpec((D, tn), lambda i, j: (0, j)),
            pl.BlockSpec((D, LANES), lambda i, j: (0, 0)),
            pl.BlockSpec((N_GATES, D), lambda i, j: (0, 0)),
            pl.BlockSpec((1, LANES), lambda i, j: (0, 0)),
            pl.BlockSpec((N_GATES, 1), lambda i, j: (0, 0)),
        ],
        out_specs=[
            pl.BlockSpec((None, tm, tn), lambda i, j: (i // nst, i % nst, j)),
            pl.BlockSpec((None, tm, LANES), lambda i, j: (i // nst, i % nst, 0)),
            pl.BlockSpec((None, N_GATES, tm), lambda i, j: (i // nst, 0, i % nst)),
        ],
        out_shape=[
            jax.ShapeDtypeStruct((B, S, NM), BF16),
            jax.ShapeDtypeStruct((B, S, LANES), F32),
            jax.ShapeDtypeStruct((B, N_GATES, S), F32),
        ],
        scratch_shapes=[pltpu.VMEM((tm, D), BF16)],
        compiler_params=_params(2),
        name="in_proj",
    )(x, g, w_main, w_gc, w_gt, b_c, b_t)


def _split2(a):
    hi = a.astype(BF16)
    lo = (a - hi.astype(F32)).astype(BF16)
    return hi, lo


def _mlstm_direction(d, p_ref, lg_ref, lgt_ref, out_ref, c_ref, n_ref, m_ref, hd):
    L = CHUNK
    r_i = lax.broadcasted_iota(jnp.int32, (L, L), 0)
    c_i = lax.broadcasted_iota(jnp.int32, (L, L), 1)
    low, up = c_i <= r_i, c_i >= r_i
    mask = up if d else low
    m_col = mask.astype(BF16)
    m_row = (low if d else up).astype(BF16)
    lg = lg_ref[...]
    lgt = lgt_ref[...]
    lg_hi, lg_lo = _split2(lg)
    a_all = _dot(m_col, lg_hi) + _dot(m_col, lg_lo)
    lgt_hi, lgt_lo = _split2(lgt)
    b_all = _dot(lgt_hi, m_row) + _dot(lgt_lo, m_row)
    inv_sqrt = hd ** -0.5
    for h in range(MLSTM_HEADS):
        idx = d * MLSTM_HEADS + h
        icol = 2 * d * MLSTM_HEADS + h
        fcol = icol + MLSTM_HEADS
        a_col = a_all[:, fcol:fcol + 1]
        r_row = lgt[icol:icol + 1, :] - b_all[fcol:fcol + 1, :]
        m_prev = m_ref[idx][0:1, 0:1]
        dmat = jnp.where(mask, a_col + r_row, NEG_BIG)
        inter = a_col + m_prev
        m_t = jnp.maximum(inter, jnp.max(dmat, axis=1, keepdims=True))
        w_intra = jnp.exp(dmat - m_t)
        w_inter = jnp.exp(inter - m_t) * inv_sqrt

        qb = p_ref[:, h * hd:(h + 1) * hd]
        kb = p_ref[:, (MLSTM_HEADS + h) * hd:(MLSTM_HEADS + h + 1) * hd]
        vb = p_ref[:, (2 * MLSTM_HEADS + h) * hd:(2 * MLSTM_HEADS + h + 1) * hd]
        s = _dot_nt(qb, kb) * (w_intra * inv_sqrt)
        c_old = c_ref[idx]
        n_old = n_ref[idx]
        num = _dot(s.astype(BF16), vb) + w_inter * _dot(qb, c_old.astype(BF16))
        qn = jnp.sum(qb.astype(F32) * n_old[0:1, :], axis=1, keepdims=True)
        den = jnp.sum(s, axis=1, keepdims=True) + w_inter * qn
        denom = jnp.maximum(jnp.abs(den), jnp.exp(-m_t))
        out_ref[:, h * hd:(h + 1) * hd] = num * (1.0 / denom)

        b_tot = jnp.sum(lgt[fcol:fcol + 1, :], axis=1, keepdims=True)
        g_row = b_tot + r_row
        m_new = jnp.maximum(b_tot + m_prev, jnp.max(g_row, axis=1, keepdims=True))
        wk_row = jnp.exp(g_row - m_new)
        decay = jnp.exp(b_tot + m_prev - m_new)
        ktw = (kb.astype(F32).T * wk_row).astype(BF16)
        c_ref[idx] = decay * c_old + _dot(ktw, vb)
        wk8 = jnp.broadcast_to(wk_row, (8, L)).astype(BF16)
        n_ref[idx] = decay * n_old + _dot(wk8, kb)
        m_ref[idx] = jnp.broadcast_to(m_new, (8, LANES))


def _mlstm_kernel(pf_ref, pb_ref, lgf_ref, lgb_ref, lgtf_ref, lgtb_ref, hf_ref, hb_ref,
                  c_ref, n_ref, m_ref, *, hd):
    @pl.when(pl.program_id(1) == 0)
    def _():
        c_ref[...] = jnp.zeros_like(c_ref)
        n_ref[...] = jnp.zeros_like(n_ref)
        m_ref[...] = jnp.zeros_like(m_ref)

    _mlstm_direction(0, pf_ref, lgf_ref, lgtf_ref, hf_ref, c_ref, n_ref, m_ref, hd)
    _mlstm_direction(1, pb_ref, lgb_ref, lgtb_ref, hb_ref, c_ref, n_ref, m_ref, hd)


def _mlstm(proj, lg, lgt, width):
    B, S, _ = proj.shape
    nc = S // CHUNK
    hd = width // MLSTM_HEADS
    fwd = lambda b, c: (b, c, 0)
    bwd = lambda b, c: (b, nc - 1 - c, 0)
    return pl.pallas_call(
        functools.partial(_mlstm_kernel, hd=hd),
        grid=(B, nc),
        in_specs=[
            pl.BlockSpec((None, CHUNK, 3 * width), fwd),
            pl.BlockSpec((None, CHUNK, 3 * width), bwd),
            pl.BlockSpec((None, CHUNK, LANES), fwd),
            pl.BlockSpec((None, CHUNK, LANES), bwd),
            pl.BlockSpec((None, N_GATES, CHUNK), lambda b, c: (b, 0, c)),
            pl.BlockSpec((None, N_GATES, CHUNK), lambda b, c: (b, 0, nc - 1 - c)),
        ],
        out_specs=[
            pl.BlockSpec((None, CHUNK, width), fwd),
            pl.BlockSpec((None, CHUNK, width), bwd),
        ],
        out_shape=[jax.ShapeDtypeStruct((B, S, width), F32)] * 2,
        scratch_shapes=[
            pltpu.VMEM((2 * MLSTM_HEADS, hd, hd), F32),
            pltpu.VMEM((2 * MLSTM_HEADS, 8, hd), F32),
            pltpu.VMEM((2 * MLSTM_HEADS, 8, LANES), F32),
        ],
        compiler_params=_params(2),
        name="mlstm",
    )(proj, proj, lg, lg, lgt, lgt)


def _attn_kernel(q_ref, k_ref, v_ref, lq1_ref, lk1_ref, lq2_ref, lk2_ref, g_ref, o_ref,
                 ka_ref, vt_ref, qat_ref, acc_ref, m_ref, r_ref, *, seq, tq, tk, lambda_init):
    head = pl.program_id(1)
    qi = pl.program_id(2)
    nk = seq // tk
    hd = LANES // 2
    slope = pltpu.bitcast(jnp.full((8, LANES), (126 - head) << 23, jnp.int32), F32)[0:1, 0:1]
    c_f32 = slope * LOG2E

    @pl.when(qi == 0)
    def _build_keys():
        def body(j, carry):
            start = pl.multiple_of(j * tk, tk)
            pos = start + lax.broadcasted_iota(jnp.int32, (tk, LANES), 0)
            lane = lax.broadcasted_iota(jnp.int32, (tk, LANES), 1)
            s_lo = pos & (LANES - 1)
            feat = jnp.where(lane < 3, s_lo, jnp.where(lane < N_FEAT, pos - s_lo, 0))
            ka_ref[j, :, 0:LANES] = k_ref[pl.ds(start, tk), :]
            ka_ref[j, :, LANES:2 * LANES] = feat.astype(F32).astype(BF16)
            vt_ref[j, 0:LANES, :] = v_ref[pl.ds(start, tk), :].astype(F32).T.astype(BF16)
            ones_row = lax.broadcasted_iota(jnp.int32, (ATTN_VROWS - LANES, tk), 0) == 0
            vt_ref[j, LANES:ATTN_VROWS, :] = jnp.where(ones_row, 1.0, 0.0).astype(BF16)
            return carry
        lax.fori_loop(0, nk, body, 0)

    q0 = qi * tq
    q_t = q_ref[...].astype(F32).T * (hd ** -0.5 * LOG2E)
    row = lax.broadcasted_iota(jnp.int32, (LANES, tq), 0)
    q_maps = (jnp.where(row < hd, q_t, 0.0).astype(BF16), jnp.where(row >= hd, q_t, 0.0).astype(BF16))
    part = jnp.zeros((LANES, tq), F32)
    for r in range(N_FEAT):
        part = jnp.where(row == r, LOG2E_PARTS[r % 3], part)
    feat_q = part * slope
    feats = (feat_q.astype(BF16), (-feat_q).astype(BF16), jnp.zeros((LANES, tq), BF16))
    for mp in range(2):
        for side in range(3):
            qat_ref[3 * mp + side, 0:LANES, :] = q_maps[mp]
            qat_ref[3 * mp + side, LANES:2 * LANES, :] = feats[side]

    t_pos = (q0 + lax.broadcasted_iota(jnp.int32, (1, tq), 1)).astype(F32)
    ct = c_f32 * t_pos
    r_ref[0] = -ct
    r_ref[1] = ct
    r_ref[2] = jnp.zeros_like(ct)

    jd = q0 // tk

    def block_of(jj):
        j = jj + (jj >= jd).astype(jnp.int32)
        return j, (j > jd).astype(jnp.int32)

    def block(idx):
        return (jd, 2) if idx == 0 else block_of(idx - 1)

    def col_max(s_t):
        run = [s_t[8 * i:8 * i + 8, :] for i in range(4)]
        for i in range(4, tk // 8):
            run[i % 4] = jnp.maximum(run[i % 4], s_t[8 * i:8 * i + 8, :])
        top = jnp.maximum(jnp.maximum(run[0], run[1]), jnp.maximum(run[2], run[3]))
        return jnp.max(top, axis=0, keepdims=True)

    def diag_bias(j):
        s_pos = j * tk + lax.broadcasted_iota(jnp.int32, (tk, tq), 0)
        t_lane = q0 + lax.broadcasted_iota(jnp.int32, (tk, tq), 1)
        return jnp.abs(s_pos - t_lane).astype(F32)

    ref = [None, None]
    seen = [[], []]
    excess = jnp.zeros((1, tq), F32)

    def scores(unit):
        idx, mp = divmod(unit, 2)
        j, side = block(idx)
        s_t = _dot(ka_ref[j], qat_ref[3 * mp + side])
        return s_t - c_f32 * diag_bias(jd) if idx == 0 else s_t

    units = 2 * nk
    pending = [scores(u) for u in range(min(ATTN_LOOKAHEAD, units))]
    for unit in range(units):
        idx, mp = divmod(unit, 2)
        j, side = block(idx)
        vt, r = vt_ref[j], r_ref[side]
        s_t = pending.pop(0)
        if unit + ATTN_LOOKAHEAD < units:
            pending.append(scores(unit + ATTN_LOOKAHEAD))
        bm = col_max(s_t) + r
        if idx == 0:
            new_ref = bm
            acc_ref[mp] = _dot(vt, jnp.exp2(s_t - (new_ref - r)).astype(BF16))
        else:
            lag = seen[mp][max(idx - 2, 0)]
            new_ref = jnp.maximum(ref[mp], lag)
            p = jnp.exp2(s_t - (new_ref - r)).astype(BF16)
            acc_ref[mp] = jnp.exp2(ref[mp] - new_ref) * acc_ref[mp] + _dot(vt, p)
            excess = jnp.maximum(excess, bm - new_ref)
        ref[mp] = new_ref
        seen[mp].append(bm)

    @pl.when(jnp.max(excess) > ATTN_MAX_EXCESS)
    def _exact_tile():
        m_ref[...] = jnp.full(m_ref.shape, NEG_BIG, F32)
        acc_ref[...] = jnp.zeros_like(acc_ref)

        def body(j, carry):
            side = jnp.where(j < jd, 0, jnp.where(j == jd, 2, 1))
            ka, vt, r = ka_ref[j], vt_ref[j], r_ref[side]
            bias = jnp.where(j == jd, c_f32, 0.0) * diag_bias(j)
            for mp in range(2):
                s_t = _dot(ka, qat_ref[3 * mp + side]) - bias
                m_old = m_ref[mp]
                m_new = jnp.maximum(m_old, col_max(s_t) + r)
                p = jnp.exp2(s_t - (m_new - r)).astype(BF16)
                acc_ref[mp] = jnp.exp2(m_old - m_new) * acc_ref[mp] + _dot(vt, p)
                m_ref[mp] = m_new
            return carry
        lax.fori_loop(0, nk, body, 0)

    lam = (jnp.exp(jnp.sum(lq1_ref[...] * lk1_ref[...], axis=1, keepdims=True))
           - jnp.exp(jnp.sum(lq2_ref[...] * lk2_ref[...], axis=1, keepdims=True)) + lambda_init)
    outs = []
    for mp in range(2):
        a = acc_ref[mp]
        outs.append(a[0:LANES, :] * (1.0 / a[LANES:LANES + 1, :]))
    o_t = outs[0] - lam * outs[1]
    ms = jnp.mean(o_t * o_t, axis=0, keepdims=True)
    y = o_t * lax.rsqrt(ms + EPS) * (g_ref[...] * (1.0 - lambda_init))
    o_ref[...] = y.T.astype(BF16)


def _attn(proj, lq1, lk1, lq2, lk2, g_col, col0, lambda_init):
    B, S, _ = proj.shape
    tq, tk = min(ATTN_TQ, S), min(ATTN_TK, S)
    nk = S // tk
    cb = col0 // LANES
    vec = pl.BlockSpec((1, LANES // 2), lambda b, h, i: (0, 0))
    return pl.pallas_call(
        functools.partial(_attn_kernel, seq=S, tq=tq, tk=tk, lambda_init=lambda_init),
        grid=(B, DIFF_HEADS, S // tq),
        in_specs=[
            pl.BlockSpec((None, tq, LANES), lambda b, h, i: (b, i, cb + h)),
            pl.BlockSpec((None, S, LANES), lambda b, h, i: (b, 0, cb + DIFF_HEADS + h)),
            pl.BlockSpec((None, S, LANES), lambda b, h, i: (b, 0, cb + 2 * DIFF_HEADS + h)),
            vec, vec, vec, vec,
            pl.BlockSpec((LANES, 1), lambda b, h, i: (0, 0)),
        ],
        out_specs=pl.BlockSpec((None, tq, LANES), lambda b, h, i: (b, i, h)),
        out_shape=jax.ShapeDtypeStruct((B, S, DIFF_HEADS * LANES), BF16),
        scratch_shapes=[
            pltpu.VMEM((nk, tk, 2 * LANES), BF16),
            pltpu.VMEM((nk, ATTN_VROWS, tk), BF16),
            pltpu.VMEM((6, 2 * LANES, tq), BF16),
            pltpu.VMEM((2, ATTN_VROWS, tq), F32),
            pltpu.VMEM((2, 1, tq), F32),
            pltpu.VMEM((3, 1, tq), F32),
        ],
        compiler_params=_params(3),
        name="attn",
    )(proj, proj, proj, lq1, lk1, lq2, lk2, g_col)


def _outproj_kernel(hf_ref, hb_ref, mo_ref, oa_ref, x_ref, gm_ref, w_ref, gp_ref, out_ref, *, hd):
    width = MLSTM_HEADS * hd
    ys = []
    for h in range(MLSTM_HEADS):
        sl = slice(h * hd, (h + 1) * hd)
        hm = _rms(hf_ref[:, sl] + hb_ref[:, sl], gm_ref[:, sl])
        ys.append((jax.nn.sigmoid(mo_ref[:, sl].astype(F32)) * hm).astype(BF16))
    ym = jnp.concatenate(ys, axis=1)
    mix = _dot(ym, w_ref[0:width, :]) + _dot(oa_ref[...], w_ref[width:, :])
    out_ref[...] = x_ref[...] + _rms(mix, gp_ref[...])


def _out_proj(hf, hb, proj, oa, x, gm, w_out, gp, width):
    B, S, D = x.shape
    tm = min(256, S)
    row = lambda b, i: (b, i, 0)
    const = lambda b, i: (0, 0)
    return pl.pallas_call(
        functools.partial(_outproj_kernel, hd=width // MLSTM_HEADS),
        grid=(B, S // tm),
        in_specs=[
            pl.BlockSpec((None, tm, width), row),
            pl.BlockSpec((None, tm, width), row),
            pl.BlockSpec((None, tm, width), lambda b, i: (b, i, 3)),
            pl.BlockSpec((None, tm, oa.shape[2]), row),
            pl.BlockSpec((None, tm, D), row),
            pl.BlockSpec((1, width), const),
            pl.BlockSpec(w_out.shape, const),
            pl.BlockSpec((1, D), const),
        ],
        out_specs=pl.BlockSpec((None, tm, D), row),
        out_shape=jax.ShapeDtypeStruct((B, S, D), F32),
        compiler_params=_params(2),
        name="out_proj",
    )(hf, hb, proj, oa, x, gm, w_out, gp)


def _mlp_kernel(x_ref, gpre_ref, wup_ref, wdown_ref, gpost_ref, out_ref, h_ref, acc_ref):
    f = pl.program_id(2)

    @pl.when(f == 0)
    def _():
        h_ref[...] = _rms(x_ref[...], gpre_ref[...]).astype(BF16)
        acc_ref[...] = jnp.zeros_like(acc_ref)

    u = jnp.maximum(_dot(h_ref[...], wup_ref[...]), 0.0)
    acc_ref[...] += _dot((u * u).astype(BF16), wdown_ref[...])

    @pl.when(f == pl.num_programs(2) - 1)
    def _():
        out_ref[...] = x_ref[...] + _rms(acc_ref[...], gpost_ref[...])


def _mlp(x, gpre, w_up, w_down, gpost):
    B, S, D = x.shape
    F = w_up.shape[1]
    tm = min(512, S)
    tf = 1024
    row = lambda b, i, f: (b, i, 0)
    const = lambda b, i, f: (0, 0)
    return pl.pallas_call(
        _mlp_kernel,
        grid=(B, S // tm, F // tf),
        in_specs=[
            pl.BlockSpec((None, tm, D), row),
            pl.BlockSpec((1, D), const),
            pl.BlockSpec((D, tf), lambda b, i, f: (0, f)),
            pl.BlockSpec((tf, D), lambda b, i, f: (f, 0)),
            pl.BlockSpec((1, D), const),
        ],
        out_specs=pl.BlockSpec((None, tm, D), row),
        out_shape=jax.ShapeDtypeStruct((B, S, D), F32),
        scratch_shapes=[pltpu.VMEM((tm, D), BF16), pltpu.VMEM((tm, D), F32)],
        compiler_params=_params(3),
        name="mlp",
    )(x, gpre, w_up, w_down, gpost)


def _layer(x, lambda_init, g_pre_mix, w_in, b_gates, mlstm_norm_g, lq1, lk1, lq2, lk2, subln_g,
           w_out, g_post_mix, g_pre_mlp, w_up, w_down, g_post_mlp):
    D = x.shape[-1]
    width = D // 2
    gate0 = 4 * width
    w_main = jnp.concatenate([w_in[:, :gate0], w_in[:, gate0 + N_GATES:]], axis=1).astype(BF16)
    w_g = w_in[:, gate0:gate0 + N_GATES]
    w_gc = jnp.pad(w_g, ((0, 0), (0, LANES - N_GATES))).astype(BF16)
    w_gt = w_g.T.astype(BF16)
    b_c = jnp.pad(b_gates, (0, LANES - N_GATES)).reshape(1, LANES)
    b_t = b_gates.reshape(N_GATES, 1)
    row = lambda v: v.reshape(1, -1)

    proj, lg, lgt = _in_proj(x, row(g_pre_mix), w_main, w_gc, w_gt, b_c, b_t)
    hf, hb = _mlstm(proj, lg, lgt, width)
    oa = _attn(proj, row(lq1), row(lk1), row(lq2), row(lk2), subln_g.reshape(-1, 1), gate0, lambda_init)
    x1 = _out_proj(hf, hb, proj, oa, x, row(mlstm_norm_g), w_out.astype(BF16), row(g_post_mix), width)
    return _mlp(x1, row(g_pre_mlp), w_up.astype(BF16), w_down.astype(BF16), row(g_post_mlp))


def _trunk(x, *weights):
    depth = weights[0].shape[0]
    for l in range(depth):
        lambda_init = 0.8 - 0.6 * math.exp(-0.3 * l)
        x = _layer(x, lambda_init, *[w[l] for w in weights])
    return x


def kernel(x_prompt, x_sample, g_pre_mix, w_in, b_gates, mlstm_norm_g, lambda_q1, lambda_k1, lambda_q2,
           lambda_k2, subln_g, w_out, g_post_mix, g_pre_mlp, w_up, w_down, g_post_mlp):
    weights = (g_pre_mix, w_in, b_gates, mlstm_norm_g, lambda_q1, lambda_k1, lambda_q2, lambda_k2,
               subln_g, w_out, g_post_mix, g_pre_mlp, w_up, w_down, g_post_mlp)
    return (_trunk(x_prompt, *weights), _trunk(x_sample, *weights))
```

```python
import functools
import math
import struct

import jax
import jax.numpy as jnp
from jax import lax
from jax.experimental import pallas as pl
from jax.experimental.pallas import tpu as pltpu

F32 = jnp.float32
BF16 = jnp.bfloat16

EPS = 1e-6
MLSTM_HEADS = 4
DIFF_HEADS = 8
CHUNK = 128
N_GATES = 4 * MLSTM_HEADS
LANES = 128
NEG_BIG = -1e30
VMEM_LIMIT = 56 * 1024 * 1024

ATTN_TQ = 256
ATTN_TK = 256
ATTN_TILES = 2
ATTN_VROWS = 144
N_FEAT = 6
ATTN_LOOKAHEAD = 4
ATTN_MAX_EXCESS = 64.0


def _f32_bits(x):
    return struct.unpack("<I", struct.pack("<f", x))[0]


def _round_f32(x):
    return struct.unpack("<f", struct.pack("<f", x))[0]


def _round_bf16(x):
    bits = _f32_bits(x)
    bits = (bits + 0x7FFF + ((bits >> 16) & 1)) & 0xFFFF0000
    return struct.unpack("<f", struct.pack("<I", bits))[0]


def _split3(value):
    parts, rest = [], value
    for _ in range(3):
        parts.append(_round_bf16(rest))
        rest = _round_f32(rest - parts[-1])
    assert rest == 0.0
    return parts


LOG2E = _round_f32(math.log2(math.e))
LOG2E_PARTS = _split3(LOG2E)


def _rms(x, g):
    return x * lax.rsqrt(jnp.mean(x * x, axis=-1, keepdims=True) + EPS) * g


def _log_sigmoid(x):
    return jnp.minimum(x, 0.0) - jnp.log1p(jnp.exp(-jnp.abs(x)))


def _dot(a, b):
    return jnp.dot(a, b, preferred_element_type=F32)


def _dot_nt(a, b):
    return lax.dot_general(a, b, (((1,), (1,)), ((), ())), preferred_element_type=F32)


def _params(n_axes, flags=None):
    return pltpu.CompilerParams(dimension_semantics=("arbitrary",) * n_axes,
                                vmem_limit_bytes=VMEM_LIMIT, flags=flags)


def _inproj_kernel(x_ref, g_ref, w_ref, wgt_ref, bt_ref, proj_ref, lg_ref, lgt_ref, hn_ref):
    @pl.when(pl.program_id(1) == 0)
    def _():
        hn = _rms(x_ref[...], g_ref[...]).astype(BF16)
        hn_ref[...] = hn
        gt = _dot_nt(wgt_ref[...], hn) + bt_ref[...]
        row = lax.broadcasted_iota(jnp.int32, gt.shape, 0)
        lgt = jnp.where((row & MLSTM_HEADS) != 0, _log_sigmoid(gt), gt)
        lgt_ref[...] = lgt
        pad = jnp.zeros((LANES - N_GATES, gt.shape[1]), F32)
        lg_ref[...] = jnp.concatenate([lgt, pad], axis=0).T

    proj_ref[...] = _dot(hn_ref[...], w_ref[...]).astype(BF16)


def _in_proj(x, g, w_main, w_gt, b_t):
    B, S, D = x.shape
    NM = w_main.shape[1]
    tm = min(1024, S)
    tn = 1024
    nst = S // tm
    return pl.pallas_call(
        _inproj_kernel,
        grid=(B * nst, NM // tn),
        in_specs=[
            pl.BlockSpec((None, tm, D), lambda i, j: (i // nst, i % nst, 0)),
            pl.BlockSpec((1, D), lambda i, j: (0, 0)),
            pl.BlockSpec((D, tn), lambda i, j: (0, j)),
            pl.BlockSpec((N_GATES, D), lambda i, j: (0, 0)),
            pl.BlockSpec((N_GATES, 1), lambda i, j: (0, 0)),
        ],
        out_specs=[
            pl.BlockSpec((None, tm, tn), lambda i, j: (i // nst, i % nst, j)),
            pl.BlockSpec((None, tm, LANES), lambda i, j: (i // nst, i % nst, 0)),
            pl.BlockSpec((None, N_GATES, tm), lambda i, j: (i // nst, 0, i % nst)),
        ],
        out_shape=[
            jax.ShapeDtypeStruct((B, S, NM), BF16),
            jax.ShapeDtypeStruct((B, S, LANES), F32),
            jax.ShapeDtypeStruct((B, N_GATES, S), F32),
        ],
        scratch_shapes=[pltpu.VMEM((tm, D), BF16)],
        compiler_params=_params(2),
        name="in_proj",
    )(x, g, w_main, w_gt, b_t)


def _split2(a):
    hi = a.astype(BF16)
    lo = (a - hi.astype(F32)).astype(BF16)
    return hi, lo


def _mlstm_direction(d, p_ref, lg_ref, lgt_ref, out_ref, c_ref, m_ref, hd):
    L = CHUNK
    r_i = lax.broadcasted_iota(jnp.int32, (L, L), 0)
    c_i = lax.broadcasted_iota(jnp.int32, (L, L), 1)
    low, up = c_i <= r_i, c_i >= r_i
    mask = up if d else low
    m_col = mask.astype(BF16)
    m_row = (low if d else up).astype(BF16)
    lg = lg_ref[...]
    lgt = lgt_ref[...]
    lg_hi, lg_lo = _split2(lg)
    a_all = _dot(m_col, lg_hi) + _dot(m_col, lg_lo)
    lgt_hi, lgt_lo = _split2(lgt)
    b_all = _dot(lgt_hi, m_row) + _dot(lgt_lo, m_row)
    inv_sqrt = hd ** -0.5
    for h in range(MLSTM_HEADS):
        idx = d * MLSTM_HEADS + h
        icol = 2 * d * MLSTM_HEADS + h
        fcol = icol + MLSTM_HEADS
        a_col = a_all[:, fcol:fcol + 1]
        r_row = lgt[icol:icol + 1, :] - b_all[fcol:fcol + 1, :]
        m_prev = m_ref[idx][0:1, 0:1]
        dmat = jnp.where(mask, a_col + r_row, NEG_BIG)
        inter = a_col + m_prev
        m_t = jnp.maximum(inter, jnp.max(dmat, axis=1, keepdims=True))
        w_intra = jnp.exp(dmat - m_t)
        w_inter = jnp.exp(inter - m_t) * inv_sqrt

        qb = p_ref[:, h * hd:(h + 1) * hd]
        kb = p_ref[:, (MLSTM_HEADS + h) * hd:(MLSTM_HEADS + h + 1) * hd]
        vb = p_ref[:, (2 * MLSTM_HEADS + h) * hd:(2 * MLSTM_HEADS + h + 1) * hd]
        v_aug = jnp.concatenate([vb, jnp.ones((L, LANES), BF16)], axis=1)
        s = _dot_nt(qb, kb) * (w_intra * inv_sqrt)
        c_old = c_ref[idx]
        num = _dot(s.astype(BF16), v_aug) + w_inter * _dot(qb, c_old.astype(BF16))
        den = num[:, hd:hd + 1]
        denom = jnp.maximum(jnp.abs(den), jnp.exp(-m_t))
        out_ref[:, h * hd:(h + 1) * hd] = num[:, 0:hd] * (1.0 / denom)

        b_tot = jnp.sum(lgt[fcol:fcol + 1, :], axis=1, keepdims=True)
        g_row = b_tot + r_row
        m_new = jnp.maximum(b_tot + m_prev, jnp.max(g_row, axis=1, keepdims=True))
        wk_row = jnp.exp(g_row - m_new)
        decay = jnp.exp(b_tot + m_prev - m_new)
        ktw = (kb.astype(F32).T * wk_row).astype(BF16)
        c_ref[idx] = decay * c_old + _dot(ktw, v_aug)
        m_ref[idx] = jnp.broadcast_to(m_new, (8, LANES))


def _mlstm_kernel(pf_ref, pb_ref, lgf_ref, lgb_ref, lgtf_ref, lgtb_ref, hf_ref, hb_ref,
                  c_ref, m_ref, *, hd):
    @pl.when(pl.program_id(1) == 0)
    def _():
        c_ref[...] = jnp.zeros_like(c_ref)
        m_ref[...] = jnp.zeros_like(m_ref)

    _mlstm_direction(0, pf_ref, lgf_ref, lgtf_ref, hf_ref, c_ref, m_ref, hd)
    _mlstm_direction(1, pb_ref, lgb_ref, lgtb_ref, hb_ref, c_ref, m_ref, hd)


def _mlstm(proj, lg, lgt, width):
    B, S, _ = proj.shape
    nc = S // CHUNK
    hd = width // MLSTM_HEADS
    fwd = lambda b, c: (b, c, 0)
    bwd = lambda b, c: (b, nc - 1 - c, 0)
    return pl.pallas_call(
        functools.partial(_mlstm_kernel, hd=hd),
        grid=(B, nc),
        in_specs=[
            pl.BlockSpec((None, CHUNK, 3 * width), fwd),
            pl.BlockSpec((None, CHUNK, 3 * width), bwd),
            pl.BlockSpec((None, CHUNK, LANES), fwd),
            pl.BlockSpec((None, CHUNK, LANES), bwd),
            pl.BlockSpec((None, N_GATES, CHUNK), lambda b, c: (b, 0, c)),
            pl.BlockSpec((None, N_GATES, CHUNK), lambda b, c: (b, 0, nc - 1 - c)),
        ],
        out_specs=[
            pl.BlockSpec((None, CHUNK, width), fwd),
            pl.BlockSpec((None, CHUNK, width), bwd),
        ],
        out_shape=[jax.ShapeDtypeStruct((B, S, width), F32)] * 2,
        scratch_shapes=[
            pltpu.VMEM((2 * MLSTM_HEADS, hd, hd + LANES), F32),
            pltpu.VMEM((2 * MLSTM_HEADS, 8, LANES), F32),
        ],
        compiler_params=_params(2),
        name="mlstm",
    )(proj, proj, lg, lg, lgt, lgt)


def _attn_kernel(q_ref, k_ref, v_ref, lq1_ref, lk1_ref, lq2_ref, lk2_ref, g_ref, o_ref,
                 ka_ref, vt_ref, qat_ref, acc_ref, m_ref, r_ref, *, seq, tq, tk, tiles, lambda_init):
    head = pl.program_id(1)
    qi = pl.program_id(2)
    nk = seq // tk
    hd = LANES // 2
    slope = pltpu.bitcast(jnp.full((8, LANES), (126 - head) << 23, jnp.int32), F32)[0:1, 0:1]
    c_f32 = slope * LOG2E

    @pl.when(qi == 0)
    def _build_keys():
        def body(j, carry):
            start = pl.multiple_of(j * tk, tk)
            pos = start + lax.broadcasted_iota(jnp.int32, (tk, LANES), 0)
            lane = lax.broadcasted_iota(jnp.int32, (tk, LANES), 1)
            s_lo = pos & (LANES - 1)
            feat = jnp.where(lane < 3, s_lo, jnp.where(lane < N_FEAT, pos - s_lo, 0))
            ka_ref[j, :, 0:LANES] = k_ref[pl.ds(start, tk), :]
            ka_ref[j, :, LANES:2 * LANES] = feat.astype(F32).astype(BF16)
            vt_ref[j, 0:LANES, :] = v_ref[pl.ds(start, tk), :].astype(F32).T.astype(BF16)
            ones_row = lax.broadcasted_iota(jnp.int32, (ATTN_VROWS - LANES, tk), 0) == 0
            vt_ref[j, LANES:ATTN_VROWS, :] = jnp.where(ones_row, 1.0, 0.0).astype(BF16)
            return carry
        lax.fori_loop(0, nk, body, 0)

    lam = (jnp.exp(jnp.sum(lq1_ref[...] * lk1_ref[...], axis=1, keepdims=True))
           - jnp.exp(jnp.sum(lq2_ref[...] * lk2_ref[...], axis=1, keepdims=True)) + lambda_init)
    row = lax.broadcasted_iota(jnp.int32, (LANES, tq), 0)
    part = jnp.zeros((LANES, tq), F32)
    for r in range(N_FEAT):
        part = jnp.where(row == r, LOG2E_PARTS[r % 3], part)
    feat_q = part * slope
    feats = (feat_q.astype(BF16), (-feat_q).astype(BF16), jnp.zeros((LANES, tq), BF16))

    q0s = [(qi * tiles + t) * tq for t in range(tiles)]
    jds = [q0 // tk for q0 in q0s]

    for t in range(tiles):
        q_t = q_ref[t * tq:(t + 1) * tq, :].astype(F32).T * (hd ** -0.5 * LOG2E)
        q_maps = (jnp.where(row < hd, q_t, 0.0).astype(BF16), jnp.where(row >= hd, q_t, 0.0).astype(BF16))
        for mp in range(2):
            for side in range(3):
                qat_ref[6 * t + 3 * mp + side, 0:LANES, :] = q_maps[mp]
                qat_ref[6 * t + 3 * mp + side, LANES:2 * LANES, :] = feats[side]
        ct = c_f32 * (q0s[t] + lax.broadcasted_iota(jnp.int32, (1, tq), 1)).astype(F32)
        r_ref[3 * t] = -ct
        r_ref[3 * t + 1] = ct
        r_ref[3 * t + 2] = jnp.zeros_like(ct)

    def block(t, idx):
        if idx == 0:
            return jds[t], 2
        j = idx - 1 + (idx - 1 >= jds[t]).astype(jnp.int32)
        return j, (j > jds[t]).astype(jnp.int32)

    def col_max(s_t):
        run = [s_t[8 * i:8 * i + 8, :] for i in range(4)]
        for i in range(4, tk // 8):
            run[i % 4] = jnp.maximum(run[i % 4], s_t[8 * i:8 * i + 8, :])
        top = jnp.maximum(jnp.maximum(run[0], run[1]), jnp.maximum(run[2], run[3]))
        return jnp.max(top, axis=0, keepdims=True)

    def diag_bias(t, j):
        s_pos = j * tk + lax.broadcasted_iota(jnp.int32, (tk, tq), 0)
        t_lane = q0s[t] + lax.broadcasted_iota(jnp.int32, (tk, tq), 1)
        return jnp.abs(s_pos - t_lane).astype(F32)

    def finalize(t):
        outs = []
        for mp in range(2):
            a = acc_ref[2 * t + mp]
            outs.append(a[0:LANES, :] * (1.0 / a[LANES:LANES + 1, :]))
        o_t = outs[0] - lam * outs[1]
        ms = jnp.mean(o_t * o_t, axis=0, keepdims=True)
        y = o_t * lax.rsqrt(ms + EPS) * (g_ref[...] * (1.0 - lambda_init))
        o_ref[t * tq:(t + 1) * tq, :] = y.T.astype(BF16)

    stream = [(t, idx, mp) for t in range(tiles) for idx in range(nk) for mp in range(2)]
    ref = [[None, None] for _ in range(tiles)]
    seen = [[[], []] for _ in range(tiles)]
    excess = [jnp.zeros((1, tq), F32) for _ in range(tiles)]

    def scores(unit):
        t, idx, mp = stream[unit]
        j, side = block(t, idx)
        s_t = _dot(ka_ref[j], qat_ref[6 * t + 3 * mp + side])
        return s_t - c_f32 * diag_bias(t, j) if idx == 0 else s_t

    pending = [scores(u) for u in range(min(ATTN_LOOKAHEAD, len(stream)))]
    for unit, (t, idx, mp) in enumerate(stream):
        j, side = block(t, idx)
        vt, r = vt_ref[j], r_ref[3 * t + side]
        s_t = pending.pop(0)
        if unit + ATTN_LOOKAHEAD < len(stream):
            pending.append(scores(unit + ATTN_LOOKAHEAD))
        bm = col_max(s_t) + r
        if idx == 0:
            new_ref = bm
            acc_ref[2 * t + mp] = _dot(vt, jnp.exp2(s_t - (new_ref - r)).astype(BF16))
        else:
            old_ref = ref[t][mp]
            new_ref = jnp.maximum(old_ref, seen[t][mp][max(idx - 2, 0)])
            p = jnp.exp2(s_t - (new_ref - r)).astype(BF16)
            acc_ref[2 * t + mp] = jnp.exp2(old_ref - new_ref) * acc_ref[2 * t + mp] + _dot(vt, p)
            excess[t] = jnp.maximum(excess[t], bm - new_ref)
        ref[t][mp] = new_ref
        seen[t][mp].append(bm)
        if idx == nk - 1 and mp == 1:
            finalize(t)

    for t in range(tiles):
        @pl.when(jnp.max(excess[t]) > ATTN_MAX_EXCESS)
        def _exact_tile(t=t):
            m_ref[...] = jnp.full(m_ref.shape, NEG_BIG, F32)
            for mp in range(2):
                acc_ref[2 * t + mp] = jnp.zeros((ATTN_VROWS, tq), F32)

            def body(j, carry):
                side = jnp.where(j < jds[t], 0, jnp.where(j == jds[t], 2, 1))
                ka, vt, r = ka_ref[j], vt_ref[j], r_ref[3 * t + side]
                bias = jnp.where(j == jds[t], c_f32, 0.0) * diag_bias(t, j)
                for mp in range(2):
                    s_t = _dot(ka, qat_ref[6 * t + 3 * mp + side]) - bias
                    m_old = m_ref[mp]
                    m_new = jnp.maximum(m_old, col_max(s_t) + r)
                    p = jnp.exp2(s_t - (m_new - r)).astype(BF16)
                    acc_ref[2 * t + mp] = jnp.exp2(m_old - m_new) * acc_ref[2 * t + mp] + _dot(vt, p)
                    m_ref[mp] = m_new
                return carry
            lax.fori_loop(0, nk, body, 0)
            finalize(t)


def _attn(proj, lq1, lk1, lq2, lk2, g_col, col0, lambda_init):
    B, S, _ = proj.shape
    tq, tk = min(ATTN_TQ, S), min(ATTN_TK, S)
    tiles = min(ATTN_TILES, S // tq)
    nk = S // tk
    cb = col0 // LANES
    vec = pl.BlockSpec((1, LANES // 2), lambda b, h, i: (0, 0))
    return pl.pallas_call(
        functools.partial(_attn_kernel, seq=S, tq=tq, tk=tk, tiles=tiles, lambda_init=lambda_init),
        grid=(B, DIFF_HEADS, S // (tiles * tq)),
        in_specs=[
            pl.BlockSpec((None, tiles * tq, LANES), lambda b, h, i: (b, i, cb + h)),
            pl.BlockSpec((None, S, LANES), lambda b, h, i: (b, 0, cb + DIFF_HEADS + h)),
            pl.BlockSpec((None, S, LANES), lambda b, h, i: (b, 0, cb + 2 * DIFF_HEADS + h)),
            vec, vec, vec, vec,
            pl.BlockSpec((LANES, 1), lambda b, h, i: (0, 0)),
        ],
        out_specs=pl.BlockSpec((None, tiles * tq, LANES), lambda b, h, i: (b, i, h)),
        out_shape=jax.ShapeDtypeStruct((B, S, DIFF_HEADS * LANES), BF16),
        scratch_shapes=[
            pltpu.VMEM((nk, tk, 2 * LANES), BF16),
            pltpu.VMEM((nk, ATTN_VROWS, tk), BF16),
            pltpu.VMEM((6 * tiles, 2 * LANES, tq), BF16),
            pltpu.VMEM((2 * tiles, ATTN_VROWS, tq), F32),
            pltpu.VMEM((2, 1, tq), F32),
            pltpu.VMEM((3 * tiles, 1, tq), F32),
        ],
        compiler_params=_params(3),
        name="attn",
    )(proj, proj, proj, lq1, lk1, lq2, lk2, g_col)


def _outproj_kernel(hf_ref, hb_ref, mo_ref, oa_ref, x_ref, gm_ref, w_ref, gp_ref, out_ref, *, hd):
    width = MLSTM_HEADS * hd
    ys = []
    for h in range(MLSTM_HEADS):
        sl = slice(h * hd, (h + 1) * hd)
        hm = _rms(hf_ref[:, sl] + hb_ref[:, sl], gm_ref[:, sl])
        ys.append((jax.nn.sigmoid(mo_ref[:, sl].astype(F32)) * hm).astype(BF16))
    ym = jnp.concatenate(ys, axis=1)
    mix = _dot(ym, w_ref[0:width, :]) + _dot(oa_ref[...], w_ref[width:, :])
    out_ref[...] = x_ref[...] + _rms(mix, gp_ref[...])


def _out_proj(hf, hb, proj, oa, x, gm, w_out, gp, width):
    B, S, D = x.shape
    tm = min(256, S)
    row = lambda b, i: (b, i, 0)
    const = lambda b, i: (0, 0)
    return pl.pallas_call(
        functools.partial(_outproj_kernel, hd=width // MLSTM_HEADS),
        grid=(B, S // tm),
        in_specs=[
            pl.BlockSpec((None, tm, width), row),
            pl.BlockSpec((None, tm, width), row),
            pl.BlockSpec((None, tm, width), lambda b, i: (b, i, 3)),
            pl.BlockSpec((None, tm, oa.shape[2]), row),
            pl.BlockSpec((None, tm, D), row),
            pl.BlockSpec((1, width), const),
            pl.BlockSpec(w_out.shape, const),
            pl.BlockSpec((1, D), const),
        ],
        out_specs=pl.BlockSpec((None, tm, D), row),
        out_shape=jax.ShapeDtypeStruct((B, S, D), F32),
        compiler_params=_params(2),
        name="out_proj",
    )(hf, hb, proj, oa, x, gm, w_out, gp)


def _mlp_kernel(x_ref, gpre_ref, wup_ref, wdown_ref, gpost_ref, out_ref, h_ref, acc_ref):
    f = pl.program_id(2)

    @pl.when(f == 0)
    def _():
        h_ref[...] = _rms(x_ref[...], gpre_ref[...]).astype(BF16)
        acc_ref[...] = jnp.zeros_like(acc_ref)

    u = jnp.maximum(_dot(h_ref[...], wup_ref[...]), 0.0)
    acc_ref[...] += _dot((u * u).astype(BF16), wdown_ref[...])

    @pl.when(f == pl.num_programs(2) - 1)
    def _():
        out_ref[...] = x_ref[...] + _rms(acc_ref[...], gpost_ref[...])


def _mlp(x, gpre, w_up, w_down, gpost):
    B, S, D = x.shape
    F = w_up.shape[1]
    tm = min(512, S)
    tf = 1024
    row = lambda b, i, f: (b, i, 0)
    const = lambda b, i, f: (0, 0)
    return pl.pallas_call(
        _mlp_kernel,
        grid=(B, S // tm, F // tf),
        in_specs=[
            pl.BlockSpec((None, tm, D), row),
            pl.BlockSpec((1, D), const),
            pl.BlockSpec((D, tf), lambda b, i, f: (0, f)),
            pl.BlockSpec((tf, D), lambda b, i, f: (f, 0)),
            pl.BlockSpec((1, D), const),
        ],
        out_specs=pl.BlockSpec((None, tm, D), row),
        out_shape=jax.ShapeDtypeStruct((B, S, D), F32),
        scratch_shapes=[pltpu.VMEM((tm, D), BF16), pltpu.VMEM((tm, D), F32)],
        compiler_params=_params(3),
        name="mlp",
    )(x, gpre, w_up, w_down, gpost)


def _layer(x, lambda_init, g_pre_mix, w_in, b_gates, mlstm_norm_g, lq1, lk1, lq2, lk2, subln_g,
           w_out, g_post_mix, g_pre_mlp, w_up, w_down, g_post_mlp):
    D = x.shape[-1]
    width = D // 2
    gate0 = 4 * width
    w_main = jnp.concatenate([w_in[:, :gate0], w_in[:, gate0 + N_GATES:]], axis=1).astype(BF16)
    w_gt = w_in[:, gate0:gate0 + N_GATES].T.astype(BF16)
    b_t = b_gates.reshape(N_GATES, 1)
    row = lambda v: v.reshape(1, -1)

    proj, lg, lgt = _in_proj(x, row(g_pre_mix), w_main, w_gt, b_t)
    hf, hb = _mlstm(proj, lg, lgt, width)
    oa = _attn(proj, row(lq1), row(lk1), row(lq2), row(lk2), subln_g.reshape(-1, 1), gate0, lambda_init)
    x1 = _out_proj(hf, hb, proj, oa, x, row(mlstm_norm_g), w_out.astype(BF16), row(g_post_mix), width)
    return _mlp(x1, row(g_pre_mlp), w_up.astype(BF16), w_down.astype(BF16), row(g_post_mlp))


def _trunk(x, *weights):
    depth = weights[0].shape[0]
    for l in range(depth):
        lambda_init = 0.8 - 0.6 * math.exp(-0.3 * l)
        x = _layer(x, lambda_init, *[w[l] for w in weights])
    return x


def kernel(x_prompt, x_sample, g_pre_mix, w_in, b_gates, mlstm_norm_g, lambda_q1, lambda_k1, lambda_q2,
           lambda_k2, subln_g, w_out, g_post_mix, g_pre_mlp, w_up, w_down, g_post_mlp):
    weights = (g_pre_mix, w_in, b_gates, mlstm_norm_g, lambda_q1, lambda_k1, lambda_q2, lambda_k2,
               subln_g, w_out, g_post_mix, g_pre_mlp, w_up, w_down, g_post_mlp)
    return (_trunk(x_prompt, *weights), _trunk(x_sample, *weights))
```

```python
import functools
import math
import struct

import jax
import jax.numpy as jnp
from jax import lax
from jax.experimental import pallas as pl
from jax.experimental.pallas import tpu as pltpu

F32 = jnp.float32
BF16 = jnp.bfloat16

EPS = 1e-6
MLSTM_HEADS = 4
DIFF_HEADS = 8
CHUNK = 128
MLSTM_LOOKAHEAD = 1
N_GATES = 4 * MLSTM_HEADS
LANES = 128
NEG_BIG = -1e30
VMEM_LIMIT = 56 * 1024 * 1024

ATTN_TQ = 256
ATTN_TK = 256
ATTN_TILES = 2
ATTN_VROWS = 144
N_FEAT = 6
ATTN_LOOKAHEAD = 4
ATTN_MAX_EXCESS = 64.0


def _f32_bits(x):
    return struct.unpack("<I", struct.pack("<f", x))[0]


def _round_f32(x):
    return struct.unpack("<f", struct.pack("<f", x))[0]


def _round_bf16(x):
    bits = _f32_bits(x)
    bits = (bits + 0x7FFF + ((bits >> 16) & 1)) & 0xFFFF0000
    return struct.unpack("<f", struct.pack("<I", bits))[0]


def _split3(value):
    parts, rest = [], value
    for _ in range(3):
        parts.append(_round_bf16(rest))
        rest = _round_f32(rest - parts[-1])
    assert rest == 0.0
    return parts


LOG2E = _round_f32(math.log2(math.e))
LOG2E_PARTS = _split3(LOG2E)


def _rms(x, g):
    return x * lax.rsqrt(jnp.mean(x * x, axis=-1, keepdims=True) + EPS) * g


def _log_sigmoid(x):
    return jnp.minimum(x, 0.0) - jnp.log1p(jnp.exp(-jnp.abs(x)))


def _dot(a, b):
    return jnp.dot(a, b, preferred_element_type=F32)


def _dot_nt(a, b):
    return lax.dot_general(a, b, (((1,), (1,)), ((), ())), preferred_element_type=F32)


def _params(n_axes, flags=None):
    return pltpu.CompilerParams(dimension_semantics=("arbitrary",) * n_axes,
                                vmem_limit_bytes=VMEM_LIMIT, flags=flags)


def _inproj_kernel(x_ref, g_ref, w_ref, wgt_ref, bt_ref, proj_ref, lg_ref, lgt_ref, hn_ref):
    @pl.when(pl.program_id(1) == 0)
    def _():
        hn = _rms(x_ref[...], g_ref[...]).astype(BF16)
        hn_ref[...] = hn
        gt = _dot_nt(wgt_ref[...], hn) + bt_ref[...]
        row = lax.broadcasted_iota(jnp.int32, gt.shape, 0)
        lgt = jnp.where((row & MLSTM_HEADS) != 0, _log_sigmoid(gt), gt)
        lgt_ref[...] = lgt
        pad = jnp.zeros((LANES - N_GATES, gt.shape[1]), F32)
        lg_ref[...] = jnp.concatenate([lgt, pad], axis=0).T

    proj_ref[...] = _dot(hn_ref[...], w_ref[...]).astype(BF16)


def _in_proj(x, g, w_main, w_gt, b_t):
    B, S, D = x.shape
    NM = w_main.shape[1]
    tm = min(1024, S)
    tn = 1024
    nst = S // tm
    return pl.pallas_call(
        _inproj_kernel,
        grid=(B * nst, NM // tn),
        in_specs=[
            pl.BlockSpec((None, tm, D), lambda i, j: (i // nst, i % nst, 0)),
            pl.BlockSpec((1, D), lambda i, j: (0, 0)),
            pl.BlockSpec((D, tn), lambda i, j: (0, j)),
            pl.BlockSpec((N_GATES, D), lambda i, j: (0, 0)),
            pl.BlockSpec((N_GATES, 1), lambda i, j: (0, 0)),
        ],
        out_specs=[
            pl.BlockSpec((None, tm, tn), lambda i, j: (i // nst, i % nst, j)),
            pl.BlockSpec((None, tm, LANES), lambda i, j: (i // nst, i % nst, 0)),
            pl.BlockSpec((None, N_GATES, tm), lambda i, j: (i // nst, 0, i % nst)),
        ],
        out_shape=[
            jax.ShapeDtypeStruct((B, S, NM), BF16),
            jax.ShapeDtypeStruct((B, S, LANES), F32),
            jax.ShapeDtypeStruct((B, N_GATES, S), F32),
        ],
        scratch_shapes=[pltpu.VMEM((tm, D), BF16)],
        compiler_params=_params(2),
        name="in_proj",
    )(x, g, w_main, w_gt, b_t)


def _split2(a):
    hi = a.astype(BF16)
    lo = (a - hi.astype(F32)).astype(BF16)
    return hi, lo


def _mlstm_gates(d, lg_ref, lgt_ref):
    L = CHUNK
    r_i = lax.broadcasted_iota(jnp.int32, (L, L), 0)
    c_i = lax.broadcasted_iota(jnp.int32, (L, L), 1)
    low, up = c_i <= r_i, c_i >= r_i
    mask = up if d else low
    m_col = mask.astype(BF16)
    m_row = (low if d else up).astype(BF16)
    lg = lg_ref[...]
    lgt = lgt_ref[...]
    lg_hi, lg_lo = _split2(lg)
    a_all = _dot(m_col, lg_hi) + _dot(m_col, lg_lo)
    lgt_hi, lgt_lo = _split2(lgt)
    b_all = _dot(lgt_hi, m_row) + _dot(lgt_lo, m_row)
    return mask, lgt, a_all, b_all


def _mlstm_scores(d, h, gates, p_ref, c_ref, m_ref, hd):
    mask, lgt, a_all, b_all = gates
    idx = d * MLSTM_HEADS + h
    icol = 2 * d * MLSTM_HEADS + h
    fcol = icol + MLSTM_HEADS
    inv_sqrt = hd ** -0.5
    a_col = a_all[:, fcol:fcol + 1]
    r_row = lgt[icol:icol + 1, :] - b_all[fcol:fcol + 1, :]
    m_prev = m_ref[idx][0:1, 0:1]
    dmat = jnp.where(mask, a_col + r_row, NEG_BIG)
    inter = a_col + m_prev
    m_t = jnp.maximum(inter, jnp.max(dmat, axis=1, keepdims=True))
    w_intra = jnp.exp(dmat - m_t) * inv_sqrt
    w_inter = jnp.exp(inter - m_t) * inv_sqrt
    qb = p_ref[:, h * hd:(h + 1) * hd]
    kb = p_ref[:, (MLSTM_HEADS + h) * hd:(MLSTM_HEADS + h + 1) * hd]
    c_old = c_ref[idx]
    b_tot = jnp.sum(lgt[fcol:fcol + 1, :], axis=1, keepdims=True)
    g_row = b_tot + r_row
    m_new = jnp.maximum(b_tot + m_prev, jnp.max(g_row, axis=1, keepdims=True))
    decay = jnp.exp(b_tot + m_prev - m_new)
    ktw = (kb.astype(F32).T * jnp.exp(g_row - m_new)).astype(BF16)
    return dict(idx=idx, h=h, m_t=m_t, w_intra=w_intra, w_inter=w_inter, c_old=c_old,
                m_new=m_new, decay=decay, ktw=ktw,
                qk=_dot_nt(qb, kb), qc=_dot(qb, c_old.astype(BF16)))


def _mlstm_output(st, p_ref, out_ref, c_ref, m_ref, hd):
    L = CHUNK
    h, idx = st["h"], st["idx"]
    vb = p_ref[:, (2 * MLSTM_HEADS + h) * hd:(2 * MLSTM_HEADS + h + 1) * hd]
    v_aug = jnp.concatenate([vb, jnp.ones((L, LANES), BF16)], axis=1)
    s = st["qk"] * st["w_intra"]
    num = _dot(s.astype(BF16), v_aug) + st["w_inter"] * st["qc"]
    denom = jnp.maximum(jnp.abs(num[:, hd:hd + 1]), jnp.exp(-st["m_t"]))
    out_ref[:, h * hd:(h + 1) * hd] = (num[:, 0:hd] * (1.0 / denom)).astype(out_ref.dtype)

    c_ref[idx] = st["decay"] * st["c_old"] + _dot(st["ktw"], v_aug)
    m_ref[idx] = jnp.broadcast_to(st["m_new"], (8, LANES))


def _mlstm_kernel(pf_ref, pb_ref, lgf_ref, lgb_ref, lgtf_ref, lgtb_ref, hf_ref, hb_ref,
                  c_ref, m_ref, *, hd):
    @pl.when(pl.program_id(1) == 0)
    def _():
        c_ref[...] = jnp.zeros_like(c_ref)
        m_ref[...] = jnp.zeros_like(m_ref)

    p_refs, out_refs = (pf_ref, pb_ref), (hf_ref, hb_ref)
    gates = (_mlstm_gates(0, lgf_ref, lgtf_ref), _mlstm_gates(1, lgb_ref, lgtb_ref))
    items = [(d, h) for h in range(MLSTM_HEADS) for d in range(2)]
    first = lambda d, h: _mlstm_scores(d, h, gates[d], p_refs[d], c_ref, m_ref, hd)
    pending = [first(*it) for it in items[:MLSTM_LOOKAHEAD]]
    for i, (d, h) in enumerate(items):
        st = pending.pop(0)
        if i + MLSTM_LOOKAHEAD < len(items):
            pending.append(first(*items[i + MLSTM_LOOKAHEAD]))
        _mlstm_output(st, p_refs[d], out_refs[d], c_ref, m_ref, hd)


def _mlstm(proj, lg, lgt, width):
    B, S, _ = proj.shape
    nc = S // CHUNK
    hd = width // MLSTM_HEADS
    fwd = lambda b, c: (b, c, 0)
    bwd = lambda b, c: (b, nc - 1 - c, 0)
    return pl.pallas_call(
        functools.partial(_mlstm_kernel, hd=hd),
        grid=(B, nc),
        in_specs=[
            pl.BlockSpec((None, CHUNK, 3 * width), fwd),
            pl.BlockSpec((None, CHUNK, 3 * width), bwd),
            pl.BlockSpec((None, CHUNK, LANES), fwd),
            pl.BlockSpec((None, CHUNK, LANES), bwd),
            pl.BlockSpec((None, N_GATES, CHUNK), lambda b, c: (b, 0, c)),
            pl.BlockSpec((None, N_GATES, CHUNK), lambda b, c: (b, 0, nc - 1 - c)),
        ],
        out_specs=[
            pl.BlockSpec((None, CHUNK, width), fwd),
            pl.BlockSpec((None, CHUNK, width), bwd),
        ],
        out_shape=[jax.ShapeDtypeStruct((B, S, width), BF16)] * 2,
        scratch_shapes=[
            pltpu.VMEM((2 * MLSTM_HEADS, hd, hd + LANES), F32),
            pltpu.VMEM((2 * MLSTM_HEADS, 8, LANES), F32),
        ],
        compiler_params=_params(2),
        name="mlstm",
    )(proj, proj, lg, lg, lgt, lgt)


def _attn_kernel(q_ref, k_ref, v_ref, lq1_ref, lk1_ref, lq2_ref, lk2_ref, g_ref, o_ref,
                 ka_ref, vt_ref, qat_ref, acc_ref, m_ref, r_ref, *, seq, tq, tk, tiles, lambda_init):
    head = pl.program_id(1)
    qi = pl.program_id(2)
    nk = seq // tk
    hd = LANES // 2
    slope = pltpu.bitcast(jnp.full((8, LANES), (126 - head) << 23, jnp.int32), F32)[0:1, 0:1]
    c_f32 = slope * LOG2E

    @pl.when(qi == 0)
    def _build_keys():
        def body(j, carry):
            start = pl.multiple_of(j * tk, tk)
            pos = start + lax.broadcasted_iota(jnp.int32, (tk, LANES), 0)
            lane = lax.broadcasted_iota(jnp.int32, (tk, LANES), 1)
            s_lo = pos & (LANES - 1)
            feat = jnp.where(lane < 3, s_lo, jnp.where(lane < N_FEAT, pos - s_lo, 0))
            ka_ref[j, :, 0:LANES] = k_ref[pl.ds(start, tk), :]
            ka_ref[j, :, LANES:2 * LANES] = feat.astype(F32).astype(BF16)
            vt_ref[j, 0:LANES, :] = v_ref[pl.ds(start, tk), :].astype(F32).T.astype(BF16)
            ones_row = lax.broadcasted_iota(jnp.int32, (ATTN_VROWS - LANES, tk), 0) == 0
            vt_ref[j, LANES:ATTN_VROWS, :] = jnp.where(ones_row, 1.0, 0.0).astype(BF16)
            return carry
        lax.fori_loop(0, nk, body, 0)

    lam = (jnp.exp(jnp.sum(lq1_ref[...] * lk1_ref[...], axis=1, keepdims=True))
           - jnp.exp(jnp.sum(lq2_ref[...] * lk2_ref[...], axis=1, keepdims=True)) + lambda_init)
    row = lax.broadcasted_iota(jnp.int32, (LANES, tq), 0)
    part = jnp.zeros((LANES, tq), F32)
    for r in range(N_FEAT):
        part = jnp.where(row == r, LOG2E_PARTS[r % 3], part)
    feat_q = part * slope
    feats = (feat_q.astype(BF16), (-feat_q).astype(BF16), jnp.zeros((LANES, tq), BF16))

    q0s = [(qi * tiles + t) * tq for t in range(tiles)]
    jds = [q0 // tk for q0 in q0s]

    for t in range(tiles):
        q_t = q_ref[t * tq:(t + 1) * tq, :].astype(F32).T * (hd ** -0.5 * LOG2E)
        q_maps = (jnp.where(row < hd, q_t, 0.0).astype(BF16), jnp.where(row >= hd, q_t, 0.0).astype(BF16))
        for mp in range(2):
            for side in range(3):
                qat_ref[6 * t + 3 * mp + side, 0:LANES, :] = q_maps[mp]
                qat_ref[6 * t + 3 * mp + side, LANES:2 * LANES, :] = feats[side]
        ct = c_f32 * (q0s[t] + lax.broadcasted_iota(jnp.int32, (1, tq), 1)).astype(F32)
        r_ref[3 * t] = -ct
        r_ref[3 * t + 1] = ct
        r_ref[3 * t + 2] = jnp.zeros_like(ct)

    def block(t, idx):
        if idx == 0:
            return jds[t], 2
        j = idx - 1 + (idx - 1 >= jds[t]).astype(jnp.int32)
        return j, (j > jds[t]).astype(jnp.int32)

    def col_max(s_t):
        run = [s_t[8 * i:8 * i + 8, :] for i in range(4)]
        for i in range(4, tk // 8):
            run[i % 4] = jnp.maximum(run[i % 4], s_t[8 * i:8 * i + 8, :])
        top = jnp.maximum(jnp.maximum(run[0], run[1]), jnp.maximum(run[2], run[3]))
        return jnp.max(top, axis=0, keepdims=True)

    def diag_bias(t, j):
        s_pos = j * tk + lax.broadcasted_iota(jnp.int32, (tk, tq), 0)
        t_lane = q0s[t] + lax.broadcasted_iota(jnp.int32, (tk, tq), 1)
        return jnp.abs(s_pos - t_lane).astype(F32)

    def finalize(t):
        outs = []
        for mp in range(2):
            a = acc_ref[2 * t + mp]
            outs.append(a[0:LANES, :] * (1.0 / a[LANES:LANES + 1, :]))
        o_t = outs[0] - lam * outs[1]
        ms = jnp.mean(o_t * o_t, axis=0, keepdims=True)
        y = o_t * lax.rsqrt(ms + EPS) * (g_ref[...] * (1.0 - lambda_init))
        o_ref[t * tq:(t + 1) * tq, :] = y.T.astype(BF16)

    stream = [(t, idx, mp) for t in range(tiles) for idx in range(nk) for mp in range(2)]
    ref = [[None, None] for _ in range(tiles)]
    seen = [[[], []] for _ in range(tiles)]
    excess = [jnp.zeros((1, tq), F32) for _ in range(tiles)]

    def scores(unit):
        t, idx, mp = stream[unit]
        j, side = block(t, idx)
        s_t = _dot(ka_ref[j], qat_ref[6 * t + 3 * mp + side])
        return s_t - c_f32 * diag_bias(t, j) if idx == 0 else s_t

    pending = [scores(u) for u in range(min(ATTN_LOOKAHEAD, len(stream)))]
    for unit, (t, idx, mp) in enumerate(stream):
        j, side = block(t, idx)
        vt, r = vt_ref[j], r_ref[3 * t + side]
        s_t = pending.pop(0)
        if unit + ATTN_LOOKAHEAD < len(stream):
            pending.append(scores(unit + ATTN_LOOKAHEAD))
        bm = col_max(s_t) + r
        if idx == 0:
            new_ref = bm
            acc_ref[2 * t + mp] = _dot(vt, jnp.exp2(s_t - (new_ref - r)).astype(BF16))
        else:
            old_ref = ref[t][mp]
            new_ref = jnp.maximum(old_ref, seen[t][mp][max(idx - 2, 0)])
            p = jnp.exp2(s_t - (new_ref - r)).astype(BF16)
            acc_ref[2 * t + mp] = jnp.exp2(old_ref - new_ref) * acc_ref[2 * t + mp] + _dot(vt, p)
            excess[t] = jnp.maximum(excess[t], bm - new_ref)
        ref[t][mp] = new_ref
        seen[t][mp].append(bm)
        if idx == nk - 1 and mp == 1:
            finalize(t)

    for t in range(tiles):
        @pl.when(jnp.max(excess[t]) > ATTN_MAX_EXCESS)
        def _exact_tile(t=t):
            m_ref[...] = jnp.full(m_ref.shape, NEG_BIG, F32)
            for mp in range(2):
                acc_ref[2 * t + mp] = jnp.zeros((ATTN_VROWS, tq), F32)

            def body(j, carry):
                side = jnp.where(j < jds[t], 0, jnp.where(j == jds[t], 2, 1))
                ka, vt, r = ka_ref[j], vt_ref[j], r_ref[3 * t + side]
                bias = jnp.where(j == jds[t], c_f32, 0.0) * diag_bias(t, j)
                for mp in range(2):
                    s_t = _dot(ka, qat_ref[6 * t + 3 * mp + side]) - bias
                    m_old = m_ref[mp]
                    m_new = jnp.maximum(m_old, col_max(s_t) + r)
                    p = jnp.exp2(s_t - (m_new - r)).astype(BF16)
                    acc_ref[2 * t + mp] = jnp.exp2(m_old - m_new) * acc_ref[2 * t + mp] + _dot(vt, p)
                    m_ref[mp] = m_new
                return carry
            lax.fori_loop(0, nk, body, 0)
            finalize(t)


def _attn(proj, lq1, lk1, lq2, lk2, g_col, col0, lambda_init):
    B, S, _ = proj.shape
    tq, tk = min(ATTN_TQ, S), min(ATTN_TK, S)
    tiles = min(ATTN_TILES, S // tq)
    nk = S // tk
    cb = col0 // LANES
    vec = pl.BlockSpec((1, LANES // 2), lambda b, h, i: (0, 0))
    return pl.pallas_call(
        functools.partial(_attn_kernel, seq=S, tq=tq, tk=tk, tiles=tiles, lambda_init=lambda_init),
        grid=(B, DIFF_HEADS, S // (tiles * tq)),
        in_specs=[
            pl.BlockSpec((None, tiles * tq, LANES), lambda b, h, i: (b, i, cb + h)),
            pl.BlockSpec((None, S, LANES), lambda b, h, i: (b, 0, cb + DIFF_HEADS + h)),
            pl.BlockSpec((None, S, LANES), lambda b, h, i: (b, 0, cb + 2 * DIFF_HEADS + h)),
            vec, vec, vec, vec,
            pl.BlockSpec((LANES, 1), lambda b, h, i: (0, 0)),
        ],
        out_specs=pl.BlockSpec((None, tiles * tq, LANES), lambda b, h, i: (b, i, h)),
        out_shape=jax.ShapeDtypeStruct((B, S, DIFF_HEADS * LANES), BF16),
        scratch_shapes=[
            pltpu.VMEM((nk, tk, 2 * LANES), BF16),
            pltpu.VMEM((nk, ATTN_VROWS, tk), BF16),
            pltpu.VMEM((6 * tiles, 2 * LANES, tq), BF16),
            pltpu.VMEM((2 * tiles, ATTN_VROWS, tq), F32),
            pltpu.VMEM((2, 1, tq), F32),
            pltpu.VMEM((3 * tiles, 1, tq), F32),
        ],
        compiler_params=_params(3),
        name="attn",
    )(proj, proj, proj, lq1, lk1, lq2, lk2, g_col)


def _outproj_kernel(hf_ref, hb_ref, mo_ref, oa_ref, x_ref, gm_ref, w_ref, gp_ref, out_ref, *, hd):
    width = MLSTM_HEADS * hd
    ys = []
    for h in range(MLSTM_HEADS):
        sl = slice(h * hd, (h + 1) * hd)
        hm = _rms(hf_ref[:, sl].astype(F32) + hb_ref[:, sl].astype(F32), gm_ref[:, sl])
        ys.append((jax.nn.sigmoid(mo_ref[:, sl].astype(F32)) * hm).astype(BF16))
    ym = jnp.concatenate(ys, axis=1)
    mix = _dot(ym, w_ref[0:width, :]) + _dot(oa_ref[...], w_ref[width:, :])
    out_ref[...] = x_ref[...] + _rms(mix, gp_ref[...])


def _out_proj(hf, hb, proj, oa, x, gm, w_out, gp, width):
    B, S, D = x.shape
    tm = min(512, S)
    row = lambda b, i: (b, i, 0)
    const = lambda b, i: (0, 0)
    return pl.pallas_call(
        functools.partial(_outproj_kernel, hd=width // MLSTM_HEADS),
        grid=(B, S // tm),
        in_specs=[
            pl.BlockSpec((None, tm, width), row),
            pl.BlockSpec((None, tm, width), row),
            pl.BlockSpec((None, tm, width), lambda b, i: (b, i, 3)),
            pl.BlockSpec((None, tm, oa.shape[2]), row),
            pl.BlockSpec((None, tm, D), row),
            pl.BlockSpec((1, width), const),
            pl.BlockSpec(w_out.shape, const),
            pl.BlockSpec((1, D), const),
        ],
        out_specs=pl.BlockSpec((None, tm, D), row),
        out_shape=jax.ShapeDtypeStruct((B, S, D), F32),
        compiler_params=_params(2),
        name="out_proj",
    )(hf, hb, proj, oa, x, gm, w_out, gp)


def _mlp_kernel(x_ref, gpre_ref, wup_ref, wdown_ref, gpost_ref, out_ref, h_ref, acc_ref):
    f = pl.program_id(2)

    @pl.when(f == 0)
    def _():
        h_ref[...] = _rms(x_ref[...], gpre_ref[...]).astype(BF16)
        acc_ref[...] = jnp.zeros_like(acc_ref)

    u = jnp.maximum(_dot(h_ref[...], wup_ref[...]), 0.0)
    acc_ref[...] += _dot((u * u).astype(BF16), wdown_ref[...])

    @pl.when(f == pl.num_programs(2) - 1)
    def _():
        out_ref[...] = x_ref[...] + _rms(acc_ref[...], gpost_ref[...])


def _mlp(x, gpre, w_up, w_down, gpost):
    B, S, D = x.shape
    F = w_up.shape[1]
    tm = min(512, S)
    tf = 1024
    row = lambda b, i, f: (b, i, 0)
    const = lambda b, i, f: (0, 0)
    return pl.pallas_call(
        _mlp_kernel,
        grid=(B, S // tm, F // tf),
        in_specs=[
            pl.BlockSpec((None, tm, D), row),
            pl.BlockSpec((1, D), const),
            pl.BlockSpec((D, tf), lambda b, i, f: (0, f)),
            pl.BlockSpec((tf, D), lambda b, i, f: (f, 0)),
            pl.BlockSpec((1, D), const),
        ],
        out_specs=pl.BlockSpec((None, tm, D), row),
        out_shape=jax.ShapeDtypeStruct((B, S, D), F32),
        scratch_shapes=[pltpu.VMEM((tm, D), BF16), pltpu.VMEM((tm, D), F32)],
        compiler_params=_params(3),
        name="mlp",
    )(x, gpre, w_up, w_down, gpost)


def _layer(x, lambda_init, g_pre_mix, w_in, b_gates, mlstm_norm_g, lq1, lk1, lq2, lk2, subln_g,
           w_out, g_post_mix, g_pre_mlp, w_up, w_down, g_post_mlp):
    D = x.shape[-1]
    width = D // 2
    gate0 = 4 * width
    w_main = jnp.concatenate([w_in[:, :gate0], w_in[:, gate0 + N_GATES:]], axis=1).astype(BF16)
    w_gt = w_in[:, gate0:gate0 + N_GATES].T.astype(BF16)
    b_t = b_gates.reshape(N_GATES, 1)
    row = lambda v: v.reshape(1, -1)

    proj, lg, lgt = _in_proj(x, row(g_pre_mix), w_main, w_gt, b_t)
    hf, hb = _mlstm(proj, lg, lgt, width)
    oa = _attn(proj, row(lq1), row(lk1), row(lq2), row(lk2), subln_g.reshape(-1, 1), gate0, lambda_init)
    x1 = _out_proj(hf, hb, proj, oa, x, row(mlstm_norm_g), w_out.astype(BF16), row(g_post_mix), width)
    return _mlp(x1, row(g_pre_mlp), w_up.astype(BF16), w_down.astype(BF16), row(g_post_mlp))


def _trunk(x, *weights):
    depth = weights[0].shape[0]
    for l in range(depth):
        lambda_init = 0.8 - 0.6 * math.exp(-0.3 * l)
        x = _layer(x, lambda_init, *[w[l] for w in weights])
    return x


def kernel(x_prompt, x_sample, g_pre_mix, w_in, b_gates, mlstm_norm_g, lambda_q1, lambda_k1, lambda_q2,
           lambda_k2, subln_g, w_out, g_post_mix, g_pre_mlp, w_up, w_down, g_post_mlp):
    weights = (g_pre_mix, w_in, b_gates, mlstm_norm_g, lambda_q1, lambda_k1, lambda_q2, lambda_k2,
               subln_g, w_out, g_post_mix, g_pre_mlp, w_up, w_down, g_post_mlp)
    return (_trunk(x_prompt, *weights), _trunk(x_sample, *weights))
```

```python
import functools
import math
import struct

import jax
import jax.numpy as jnp
from jax import lax
from jax.experimental import pallas as pl
from jax.experimental.pallas import tpu as pltpu

F32 = jnp.float32
BF16 = jnp.bfloat16

EPS = 1e-6
MLSTM_HEADS = 4
DIFF_HEADS = 8
CHUNK = 128
MLSTM_LOOKAHEAD = 1
N_GATES = 4 * MLSTM_HEADS
LANES = 128
NEG_BIG = -1e30
VMEM_LIMIT = 56 * 1024 * 1024

ATTN_TQ = 256
ATTN_TK = 256
ATTN_STREAM_BLOCKS = 64
ATTN_VROWS = 144
N_FEAT = 6
ATTN_LOOKAHEAD = 4
ATTN_MAX_EXCESS = 64.0


def _f32_bits(x):
    return struct.unpack("<I", struct.pack("<f", x))[0]


def _round_f32(x):
    return struct.unpack("<f", struct.pack("<f", x))[0]


def _round_bf16(x):
    bits = _f32_bits(x)
    bits = (bits + 0x7FFF + ((bits >> 16) & 1)) & 0xFFFF0000
    return struct.unpack("<f", struct.pack("<I", bits))[0]


def _split3(value):
    parts, rest = [], value
    for _ in range(3):
        parts.append(_round_bf16(rest))
        rest = _round_f32(rest - parts[-1])
    assert rest == 0.0
    return parts


LOG2E = _round_f32(math.log2(math.e))
LOG2E_PARTS = _split3(LOG2E)


def _rms(x, g):
    return x * lax.rsqrt(jnp.mean(x * x, axis=-1, keepdims=True) + EPS) * g


def _log_sigmoid(x):
    return jnp.minimum(x, 0.0) - jnp.log1p(jnp.exp(-jnp.abs(x)))


def _dot(a, b):
    return jnp.dot(a, b, preferred_element_type=F32)


def _dot_nt(a, b):
    return lax.dot_general(a, b, (((1,), (1,)), ((), ())), preferred_element_type=F32)


def _params(n_axes, flags=None):
    return pltpu.CompilerParams(dimension_semantics=("arbitrary",) * n_axes,
                                vmem_limit_bytes=VMEM_LIMIT, flags=flags)


def _inproj_kernel(x_ref, g_ref, w_ref, wgt_ref, bt_ref, proj_ref, lg_ref, lgt_ref, hn_ref):
    @pl.when(pl.program_id(1) == 0)
    def _():
        hn = _rms(x_ref[...], g_ref[...]).astype(BF16)
        hn_ref[...] = hn
        gt = _dot_nt(wgt_ref[...], hn) + bt_ref[...]
        row = lax.broadcasted_iota(jnp.int32, gt.shape, 0)
        lgt = jnp.where((row & MLSTM_HEADS) != 0, _log_sigmoid(gt), gt)
        lgt_ref[...] = lgt
        pad = jnp.zeros((LANES - N_GATES, gt.shape[1]), F32)
        lg_ref[...] = jnp.concatenate([lgt, pad], axis=0).T

    proj_ref[...] = _dot(hn_ref[...], w_ref[...]).astype(BF16)


def _in_proj(x, g, w_main, w_gt, b_t):
    B, S, D = x.shape
    NM = w_main.shape[1]
    tm = min(1024, S)
    tn = 1024
    nst = S // tm
    return pl.pallas_call(
        _inproj_kernel,
        grid=(B * nst, NM // tn),
        in_specs=[
            pl.BlockSpec((None, tm, D), lambda i, j: (i // nst, i % nst, 0)),
            pl.BlockSpec((1, D), lambda i, j: (0, 0)),
            pl.BlockSpec((D, tn), lambda i, j: (0, j)),
            pl.BlockSpec((N_GATES, D), lambda i, j: (0, 0)),
            pl.BlockSpec((N_GATES, 1), lambda i, j: (0, 0)),
        ],
        out_specs=[
            pl.BlockSpec((None, tm, tn), lambda i, j: (i // nst, i % nst, j)),
            pl.BlockSpec((None, tm, LANES), lambda i, j: (i // nst, i % nst, 0)),
            pl.BlockSpec((None, N_GATES, tm), lambda i, j: (i // nst, 0, i % nst)),
        ],
        out_shape=[
            jax.ShapeDtypeStruct((B, S, NM), BF16),
            jax.ShapeDtypeStruct((B, S, LANES), F32),
            jax.ShapeDtypeStruct((B, N_GATES, S), F32),
        ],
        scratch_shapes=[pltpu.VMEM((tm, D), BF16)],
        compiler_params=_params(2),
        name="in_proj",
    )(x, g, w_main, w_gt, b_t)


def _split2(a):
    hi = a.astype(BF16)
    lo = (a - hi.astype(F32)).astype(BF16)
    return hi, lo


def _mlstm_gates(d, lg_ref, lgt_ref):
    L = CHUNK
    r_i = lax.broadcasted_iota(jnp.int32, (L, L), 0)
    c_i = lax.broadcasted_iota(jnp.int32, (L, L), 1)
    low, up = c_i <= r_i, c_i >= r_i
    mask = up if d else low
    m_col = mask.astype(BF16)
    m_row = (low if d else up).astype(BF16)
    lg = lg_ref[...]
    lgt = lgt_ref[...]
    lg_hi, lg_lo = _split2(lg)
    a_all = _dot(m_col, lg_hi) + _dot(m_col, lg_lo)
    lgt_hi, lgt_lo = _split2(lgt)
    b_all = _dot(lgt_hi, m_row) + _dot(lgt_lo, m_row)
    return mask, lgt, a_all, b_all


def _mlstm_scores(d, h, gates, p_ref, c_ref, m_ref, hd):
    mask, lgt, a_all, b_all = gates
    idx = d * MLSTM_HEADS + h
    icol = 2 * d * MLSTM_HEADS + h
    fcol = icol + MLSTM_HEADS
    inv_sqrt = hd ** -0.5
    a_col = a_all[:, fcol:fcol + 1]
    r_row = lgt[icol:icol + 1, :] - b_all[fcol:fcol + 1, :]
    m_prev = m_ref[idx][0:1, 0:1]
    dmat = jnp.where(mask, a_col + r_row, NEG_BIG)
    inter = a_col + m_prev
    m_t = jnp.maximum(inter, jnp.max(dmat, axis=1, keepdims=True))
    w_intra = jnp.exp(dmat - m_t) * inv_sqrt
    w_inter = jnp.exp(inter - m_t) * inv_sqrt
    qb = p_ref[:, h * hd:(h + 1) * hd]
    kb = p_ref[:, (MLSTM_HEADS + h) * hd:(MLSTM_HEADS + h + 1) * hd]
    c_old = c_ref[idx]
    b_tot = jnp.sum(lgt[fcol:fcol + 1, :], axis=1, keepdims=True)
    g_row = b_tot + r_row
    m_new = jnp.maximum(b_tot + m_prev, jnp.max(g_row, axis=1, keepdims=True))
    decay = jnp.exp(b_tot + m_prev - m_new)
    ktw = (kb.astype(F32).T * jnp.exp(g_row - m_new)).astype(BF16)
    return dict(idx=idx, h=h, m_t=m_t, w_intra=w_intra, w_inter=w_inter, c_old=c_old,
                m_new=m_new, decay=decay, ktw=ktw,
                qk=_dot_nt(qb, kb), qc=_dot(qb, c_old.astype(BF16)))


def _mlstm_output(st, p_ref, out_ref, c_ref, m_ref, hd):
    L = CHUNK
    h, idx = st["h"], st["idx"]
    vb = p_ref[:, (2 * MLSTM_HEADS + h) * hd:(2 * MLSTM_HEADS + h + 1) * hd]
    v_aug = jnp.concatenate([vb, jnp.ones((L, LANES), BF16)], axis=1)
    s = st["qk"] * st["w_intra"]
    num = _dot(s.astype(BF16), v_aug) + st["w_inter"] * st["qc"]
    denom = jnp.maximum(jnp.abs(num[:, hd:hd + 1]), jnp.exp(-st["m_t"]))
    out_ref[:, h * hd:(h + 1) * hd] = (num[:, 0:hd] * (1.0 / denom)).astype(out_ref.dtype)

    c_ref[idx] = st["decay"] * st["c_old"] + _dot(st["ktw"], v_aug)
    m_ref[idx] = jnp.broadcast_to(st["m_new"], (8, LANES))


def _mlstm_kernel(pf_ref, pb_ref, lgf_ref, lgb_ref, lgtf_ref, lgtb_ref, hf_ref, hb_ref,
                  c_ref, m_ref, *, hd):
    @pl.when(pl.program_id(1) == 0)
    def _():
        c_ref[...] = jnp.zeros_like(c_ref)
        m_ref[...] = jnp.zeros_like(m_ref)

    p_refs, out_refs = (pf_ref, pb_ref), (hf_ref, hb_ref)
    gates = (_mlstm_gates(0, lgf_ref, lgtf_ref), _mlstm_gates(1, lgb_ref, lgtb_ref))
    items = [(d, h) for h in range(MLSTM_HEADS) for d in range(2)]
    first = lambda d, h: _mlstm_scores(d, h, gates[d], p_refs[d], c_ref, m_ref, hd)
    pending = [first(*it) for it in items[:MLSTM_LOOKAHEAD]]
    for i, (d, h) in enumerate(items):
        st = pending.pop(0)
        if i + MLSTM_LOOKAHEAD < len(items):
            pending.append(first(*items[i + MLSTM_LOOKAHEAD]))
        _mlstm_output(st, p_refs[d], out_refs[d], c_ref, m_ref, hd)


def _mlstm(proj, lg, lgt, width):
    B, S, _ = proj.shape
    nc = S // CHUNK
    hd = width // MLSTM_HEADS
    fwd = lambda b, c: (b, c, 0)
    bwd = lambda b, c: (b, nc - 1 - c, 0)
    return pl.pallas_call(
        functools.partial(_mlstm_kernel, hd=hd),
        grid=(B, nc),
        in_specs=[
            pl.BlockSpec((None, CHUNK, 3 * width), fwd),
            pl.BlockSpec((None, CHUNK, 3 * width), bwd),
            pl.BlockSpec((None, CHUNK, LANES), fwd),
            pl.BlockSpec((None, CHUNK, LANES), bwd),
            pl.BlockSpec((None, N_GATES, CHUNK), lambda b, c: (b, 0, c)),
            pl.BlockSpec((None, N_GATES, CHUNK), lambda b, c: (b, 0, nc - 1 - c)),
        ],
        out_specs=[
            pl.BlockSpec((None, CHUNK, width), fwd),
            pl.BlockSpec((None, CHUNK, width), bwd),
        ],
        out_shape=[jax.ShapeDtypeStruct((B, S, width), BF16)] * 2,
        scratch_shapes=[
            pltpu.VMEM((2 * MLSTM_HEADS, hd, hd + LANES), F32),
            pltpu.VMEM((2 * MLSTM_HEADS, 8, LANES), F32),
        ],
        compiler_params=_params(2),
        name="mlstm",
    )(proj, proj, lg, lg, lgt, lgt)


def _attn_kernel(q_ref, k_ref, v_ref, lq1_ref, lk1_ref, lq2_ref, lk2_ref, g_ref, o_ref,
                 ka_ref, vt_ref, qat_ref, acc_ref, m_ref, r_ref, *, seq, tq, tk, tiles, lambda_init):
    head = pl.program_id(1)
    qi = pl.program_id(2)
    nk = seq // tk
    hd = LANES // 2
    slope = pltpu.bitcast(jnp.full((8, LANES), (126 - head) << 23, jnp.int32), F32)[0:1, 0:1]
    c_f32 = slope * LOG2E

    @pl.when((pl.program_id(0) == 0) & (head == 0) & (qi == 0))
    def _build_constants():
        def body(j, carry):
            pos = j * tk + lax.broadcasted_iota(jnp.int32, (tk, LANES), 0)
            lane = lax.broadcasted_iota(jnp.int32, (tk, LANES), 1)
            s_lo = pos & (LANES - 1)
            feat = jnp.where(lane < 3, s_lo, jnp.where(lane < N_FEAT, pos - s_lo, 0))
            ka_ref[j, :, LANES:2 * LANES] = feat.astype(F32).astype(BF16)
            ones_row = lax.broadcasted_iota(jnp.int32, (ATTN_VROWS - LANES, tk), 0) == 0
            vt_ref[j, LANES:ATTN_VROWS, :] = jnp.where(ones_row, 1.0, 0.0).astype(BF16)
            return carry
        lax.fori_loop(0, nk, body, 0)

    @pl.when(qi == 0)
    def _build_keys():
        def body(j, carry):
            start = pl.multiple_of(j * tk, tk)
            ka_ref[j, :, 0:LANES] = k_ref[pl.ds(start, tk), :]
            vt_ref[j, 0:LANES, :] = v_ref[pl.ds(start, tk), :].astype(F32).T.astype(BF16)
            return carry
        lax.fori_loop(0, nk, body, 0)

    lam = (jnp.exp(jnp.sum(lq1_ref[...] * lk1_ref[...], axis=1, keepdims=True))
           - jnp.exp(jnp.sum(lq2_ref[...] * lk2_ref[...], axis=1, keepdims=True)) + lambda_init)
    row = lax.broadcasted_iota(jnp.int32, (LANES, tq), 0)
    part = jnp.zeros((LANES, tq), F32)
    for r in range(N_FEAT):
        part = jnp.where(row == r, LOG2E_PARTS[r % 3], part)
    feat_q = part * slope
    feats = (feat_q.astype(BF16), (-feat_q).astype(BF16), jnp.zeros((LANES, tq), BF16))

    q0s = [(qi * tiles + t) * tq for t in range(tiles)]
    jds = [q0 // tk for q0 in q0s]

    for t in range(tiles):
        q_t = q_ref[t * tq:(t + 1) * tq, :].astype(F32).T * (hd ** -0.5 * LOG2E)
        q_maps = (jnp.where(row < hd, q_t, 0.0).astype(BF16), jnp.where(row >= hd, q_t, 0.0).astype(BF16))
        for mp in range(2):
            for side in range(3):
                qat_ref[6 * t + 3 * mp + side, 0:LANES, :] = q_maps[mp]
                qat_ref[6 * t + 3 * mp + side, LANES:2 * LANES, :] = feats[side]
        ct = c_f32 * (q0s[t] + lax.broadcasted_iota(jnp.int32, (1, tq), 1)).astype(F32)
        r_ref[3 * t] = -ct
        r_ref[3 * t + 1] = ct
        r_ref[3 * t + 2] = jnp.zeros_like(ct)

    def block(t, idx):
        if idx == 0:
            return jds[t], 2
        j = idx - 1 + (idx - 1 >= jds[t]).astype(jnp.int32)
        return j, (j > jds[t]).astype(jnp.int32)

    def col_max(s_t):
        run = [s_t[8 * i:8 * i + 8, :] for i in range(4)]
        for i in range(4, tk // 8):
            run[i % 4] = jnp.maximum(run[i % 4], s_t[8 * i:8 * i + 8, :])
        top = jnp.maximum(jnp.maximum(run[0], run[1]), jnp.maximum(run[2], run[3]))
        return jnp.max(top, axis=0, keepdims=True)

    def diag_bias(t, j):
        s_pos = j * tk + lax.broadcasted_iota(jnp.int32, (tk, tq), 0)
        t_lane = q0s[t] + lax.broadcasted_iota(jnp.int32, (tk, tq), 1)
        return jnp.abs(s_pos - t_lane).astype(F32)

    def finalize(t):
        outs = []
        for mp in range(2):
            a = acc_ref[2 * t + mp]
            outs.append(a[0:LANES, :] * (1.0 / a[LANES:LANES + 1, :]))
        o_t = outs[0] - lam * outs[1]
        ms = jnp.mean(o_t * o_t, axis=0, keepdims=True)
        y = o_t * lax.rsqrt(ms + EPS) * (g_ref[...] * (1.0 - lambda_init))
        o_ref[t * tq:(t + 1) * tq, :] = y.T.astype(BF16)

    stream = [(t, idx, mp) for t in range(tiles) for idx in range(nk) for mp in range(2)]
    ref = [[None, None] for _ in range(tiles)]
    seen = [[[], []] for _ in range(tiles)]
    excess = [jnp.zeros((1, tq), F32) for _ in range(tiles)]

    def scores(unit):
        t, idx, mp = stream[unit]
        j, side = block(t, idx)
        s_t = _dot(ka_ref[j], qat_ref[6 * t + 3 * mp + side])
        return s_t - c_f32 * diag_bias(t, j) if idx == 0 else s_t

    pending = [scores(u) for u in range(min(ATTN_LOOKAHEAD, len(stream)))]
    for unit, (t, idx, mp) in enumerate(stream):
        j, side = block(t, idx)
        vt, r = vt_ref[j], r_ref[3 * t + side]
        s_t = pending.pop(0)
        if unit + ATTN_LOOKAHEAD < len(stream):
            pending.append(scores(unit + ATTN_LOOKAHEAD))
        bm = col_max(s_t) + r
        if idx == 0:
            new_ref = bm
            acc_ref[2 * t + mp] = _dot(vt, jnp.exp2(s_t - (new_ref - r)).astype(BF16))
        else:
            old_ref = ref[t][mp]
            new_ref = jnp.maximum(old_ref, seen[t][mp][max(idx - 2, 0)])
            p = jnp.exp2(s_t - (new_ref - r)).astype(BF16)
            acc_ref[2 * t + mp] = jnp.exp2(old_ref - new_ref) * acc_ref[2 * t + mp] + _dot(vt, p)
            excess[t] = jnp.maximum(excess[t], bm - new_ref)
        ref[t][mp] = new_ref
        seen[t][mp].append(bm)
        if idx == nk - 1 and mp == 1:
            finalize(t)

    for t in range(tiles):
        @pl.when(jnp.max(excess[t]) > ATTN_MAX_EXCESS)
        def _exact_tile(t=t):
            m_ref[...] = jnp.full(m_ref.shape, NEG_BIG, F32)
            for mp in range(2):
                acc_ref[2 * t + mp] = jnp.zeros((ATTN_VROWS, tq), F32)

            def body(j, carry):
                side = jnp.where(j < jds[t], 0, jnp.where(j == jds[t], 2, 1))
                ka, vt, r = ka_ref[j], vt_ref[j], r_ref[3 * t + side]
                bias = jnp.where(j == jds[t], c_f32, 0.0) * diag_bias(t, j)
                for mp in range(2):
                    s_t = _dot(ka, qat_ref[6 * t + 3 * mp + side]) - bias
                    m_old = m_ref[mp]
                    m_new = jnp.maximum(m_old, col_max(s_t) + r)
                    p = jnp.exp2(s_t - (m_new - r)).astype(BF16)
                    acc_ref[2 * t + mp] = jnp.exp2(m_old - m_new) * acc_ref[2 * t + mp] + _dot(vt, p)
                    m_ref[mp] = m_new
                return carry
            lax.fori_loop(0, nk, body, 0)
            finalize(t)


def _attn(proj, lq1, lk1, lq2, lk2, g_col, col0, lambda_init):
    B, S, _ = proj.shape
    tq, tk = min(ATTN_TQ, S), min(ATTN_TK, S)
    nk = S // tk
    tiles = max(1, min(ATTN_STREAM_BLOCKS // nk, S // tq))
    cb = col0 // LANES
    vec = pl.BlockSpec((1, LANES // 2), lambda b, h, i: (0, 0))
    return pl.pallas_call(
        functools.partial(_attn_kernel, seq=S, tq=tq, tk=tk, tiles=tiles, lambda_init=lambda_init),
        grid=(B, DIFF_HEADS, S // (tiles * tq)),
        in_specs=[
            pl.BlockSpec((None, tiles * tq, LANES), lambda b, h, i: (b, i, cb + h)),
            pl.BlockSpec((None, S, LANES), lambda b, h, i: (b, 0, cb + DIFF_HEADS + h)),
            pl.BlockSpec((None, S, LANES), lambda b, h, i: (b, 0, cb + 2 * DIFF_HEADS + h)),
            vec, vec, vec, vec,
            pl.BlockSpec((LANES, 1), lambda b, h, i: (0, 0)),
        ],
        out_specs=pl.BlockSpec((None, tiles * tq, LANES), lambda b, h, i: (b, i, h)),
        out_shape=jax.ShapeDtypeStruct((B, S, DIFF_HEADS * LANES), BF16),
        scratch_shapes=[
            pltpu.VMEM((nk, tk, 2 * LANES), BF16),
            pltpu.VMEM((nk, ATTN_VROWS, tk), BF16),
            pltpu.VMEM((6 * tiles, 2 * LANES, tq), BF16),
            pltpu.VMEM((2 * tiles, ATTN_VROWS, tq), F32),
            pltpu.VMEM((2, 1, tq), F32),
            pltpu.VMEM((3 * tiles, 1, tq), F32),
        ],
        compiler_params=_params(3),
        name="attn",
    )(proj, proj, proj, lq1, lk1, lq2, lk2, g_col)


def _outproj_kernel(hf_ref, hb_ref, mo_ref, oa_ref, x_ref, gm_ref, w_ref, gp_ref, out_ref, *, hd):
    width = MLSTM_HEADS * hd
    ys = []
    for h in range(MLSTM_HEADS):
        sl = slice(h * hd, (h + 1) * hd)
        hm = _rms(hf_ref[:, sl].astype(F32) + hb_ref[:, sl].astype(F32), gm_ref[:, sl])
        ys.append((jax.nn.sigmoid(mo_ref[:, sl].astype(F32)) * hm).astype(BF16))
    ym = jnp.concatenate(ys, axis=1)
    mix = _dot(ym, w_ref[0:width, :]) + _dot(oa_ref[...], w_ref[width:, :])
    out_ref[...] = x_ref[...] + _rms(mix, gp_ref[...])


def _out_proj(hf, hb, proj, oa, x, gm, w_out, gp, width):
    B, S, D = x.shape
    tm = min(512, S)
    row = lambda b, i: (b, i, 0)
    const = lambda b, i: (0, 0)
    return pl.pallas_call(
        functools.partial(_outproj_kernel, hd=width // MLSTM_HEADS),
        grid=(B, S // tm),
        in_specs=[
            pl.BlockSpec((None, tm, width), row),
            pl.BlockSpec((None, tm, width), row),
            pl.BlockSpec((None, tm, width), lambda b, i: (b, i, 3)),
            pl.BlockSpec((None, tm, oa.shape[2]), row),
            pl.BlockSpec((None, tm, D), row),
            pl.BlockSpec((1, width), const),
            pl.BlockSpec(w_out.shape, const),
            pl.BlockSpec((1, D), const),
        ],
        out_specs=pl.BlockSpec((None, tm, D), row),
        out_shape=jax.ShapeDtypeStruct((B, S, D), F32),
        compiler_params=_params(2),
        name="out_proj",
    )(hf, hb, proj, oa, x, gm, w_out, gp)


def _mlp_kernel(x_ref, gpre_ref, wup_ref, wdown_ref, gpost_ref, out_ref, h_ref, acc_ref):
    f = pl.program_id(2)

    @pl.when(f == 0)
    def _():
        h_ref[...] = _rms(x_ref[...], gpre_ref[...]).astype(BF16)
        acc_ref[...] = jnp.zeros_like(acc_ref)

    u = jnp.maximum(_dot(h_ref[...], wup_ref[...]), 0.0)
    acc_ref[...] += _dot((u * u).astype(BF16), wdown_ref[...])

    @pl.when(f == pl.num_programs(2) - 1)
    def _():
        out_ref[...] = x_ref[...] + _rms(acc_ref[...], gpost_ref[...])


def _mlp(x, gpre, w_up, w_down, gpost):
    B, S, D = x.shape
    F = w_up.shape[1]
    tm = min(512, S)
    tf = 1024
    row = lambda b, i, f: (b, i, 0)
    const = lambda b, i, f: (0, 0)
    return pl.pallas_call(
        _mlp_kernel,
        grid=(B, S // tm, F // tf),
        in_specs=[
            pl.BlockSpec((None, tm, D), row),
            pl.BlockSpec((1, D), const),
            pl.BlockSpec((D, tf), lambda b, i, f: (0, f)),
            pl.BlockSpec((tf, D), lambda b, i, f: (f, 0)),
            pl.BlockSpec((1, D), const),
        ],
        out_specs=pl.BlockSpec((None, tm, D), row),
        out_shape=jax.ShapeDtypeStruct((B, S, D), F32),
        scratch_shapes=[pltpu.VMEM((tm, D), BF16), pltpu.VMEM((tm, D), F32)],
        compiler_params=_params(3),
        name="mlp",
    )(x, gpre, w_up, w_down, gpost)


def _layer(x, lambda_init, g_pre_mix, w_in, b_gates, mlstm_norm_g, lq1, lk1, lq2, lk2, subln_g,
           w_out, g_post_mix, g_pre_mlp, w_up, w_down, g_post_mlp):
    D = x.shape[-1]
    width = D // 2
    gate0 = 4 * width
    w_main = jnp.concatenate([w_in[:, :gate0], w_in[:, gate0 + N_GATES:]], axis=1).astype(BF16)
    w_gt = w_in[:, gate0:gate0 + N_GATES].T.astype(BF16)
    b_t = b_gates.reshape(N_GATES, 1)
    row = lambda v: v.reshape(1, -1)

    proj, lg, lgt = _in_proj(x, row(g_pre_mix), w_main, w_gt, b_t)
    hf, hb = _mlstm(proj, lg, lgt, width)
    oa = _attn(proj, row(lq1), row(lk1), row(lq2), row(lk2), subln_g.reshape(-1, 1), gate0, lambda_init)
    x1 = _out_proj(hf, hb, proj, oa, x, row(mlstm_norm_g), w_out.astype(BF16), row(g_post_mix), width)
    return _mlp(x1, row(g_pre_mlp), w_up.astype(BF16), w_down.astype(BF16), row(g_post_mlp))


def _trunk(x, *weights):
    depth = weights[0].shape[0]
    for l in range(depth):
        lambda_init = 0.8 - 0.6 * math.exp(-0.3 * l)
        x = _layer(x, lambda_init, *[w[l] for w in weights])
    return x


def kernel(x_prompt, x_sample, g_pre_mix, w_in, b_gates, mlstm_norm_g, lambda_q1, lambda_k1, lambda_q2,
           lambda_k2, subln_g, w_out, g_post_mix, g_pre_mlp, w_up, w_down, g_post_mlp):
    weights = (g_pre_mix, w_in, b_gates, mlstm_norm_g, lambda_q1, lambda_k1, lambda_q2, lambda_k2,
               subln_g, w_out, g_post_mix, g_pre_mlp, w_up, w_down, g_post_mlp)
    return (_trunk(x_prompt, *weights), _trunk(x_sample, *weights))
```

```python
import functools
import math
import struct

import jax
import jax.numpy as jnp
from jax import lax
from jax.experimental import pallas as pl
from jax.experimental.pallas import tpu as pltpu

F32 = jnp.float32
BF16 = jnp.bfloat16

EPS = 1e-6
MLSTM_HEADS = 4
DIFF_HEADS = 8
CHUNK = 256
MLSTM_LOOKAHEAD = 2
N_GATES = 4 * MLSTM_HEADS
LANES = 128
NEG_BIG = -1e30
VMEM_LIMIT = 56 * 1024 * 1024

ATTN_TQ = 256
ATTN_TK = 256
ATTN_STREAM_BLOCKS = 64
ATTN_VROWS = 144
N_FEAT = 6
ATTN_LOOKAHEAD = 4
ATTN_MAX_EXCESS = 64.0


def _f32_bits(x):
    return struct.unpack("<I", struct.pack("<f", x))[0]


def _round_f32(x):
    return struct.unpack("<f", struct.pack("<f", x))[0]


def _round_bf16(x):
    bits = _f32_bits(x)
    bits = (bits + 0x7FFF + ((bits >> 16) & 1)) & 0xFFFF0000
    return struct.unpack("<f", struct.pack("<I", bits))[0]


def _split3(value):
    parts, rest = [], value
    for _ in range(3):
        parts.append(_round_bf16(rest))
        rest = _round_f32(rest - parts[-1])
    assert rest == 0.0
    return parts


LOG2E = _round_f32(math.log2(math.e))
LOG2E_PARTS = _split3(LOG2E)


def _rms(x, g):
    return x * lax.rsqrt(jnp.mean(x * x, axis=-1, keepdims=True) + EPS) * g


def _log_sigmoid(x):
    return jnp.minimum(x, 0.0) - jnp.log1p(jnp.exp(-jnp.abs(x)))


def _dot(a, b):
    return jnp.dot(a, b, preferred_element_type=F32)


def _dot_nt(a, b):
    return lax.dot_general(a, b, (((1,), (1,)), ((), ())), preferred_element_type=F32)


def _params(n_axes, flags=None):
    return pltpu.CompilerParams(dimension_semantics=("arbitrary",) * n_axes,
                                vmem_limit_bytes=VMEM_LIMIT, flags=flags)


def _inproj_kernel(x_ref, g_ref, w_ref, wgt_ref, bt_ref, proj_ref, lg_ref, lgt_ref, hn_ref):
    @pl.when(pl.program_id(1) == 0)
    def _():
        hn = _rms(x_ref[...], g_ref[...]).astype(BF16)
        hn_ref[...] = hn
        gt = _dot_nt(wgt_ref[...], hn) + bt_ref[...]
        row = lax.broadcasted_iota(jnp.int32, gt.shape, 0)
        lgt = jnp.where((row & MLSTM_HEADS) != 0, _log_sigmoid(gt), gt)
        lgt_ref[...] = lgt
        pad = jnp.zeros((LANES - N_GATES, gt.shape[1]), F32)
        lg_ref[...] = jnp.concatenate([lgt, pad], axis=0).T

    proj_ref[...] = _dot(hn_ref[...], w_ref[...]).astype(BF16)


def _in_proj(x, g, w_main, w_gt, b_t):
    B, S, D = x.shape
    NM = w_main.shape[1]
    tm = min(1024, S)
    tn = NM // 4
    nst = S // tm
    return pl.pallas_call(
        _inproj_kernel,
        grid=(B * nst, NM // tn),
        in_specs=[
            pl.BlockSpec((None, tm, D), lambda i, j: (i // nst, i % nst, 0)),
            pl.BlockSpec((1, D), lambda i, j: (0, 0)),
            pl.BlockSpec((D, tn), lambda i, j: (0, j)),
            pl.BlockSpec((N_GATES, D), lambda i, j: (0, 0)),
            pl.BlockSpec((N_GATES, 1), lambda i, j: (0, 0)),
        ],
        out_specs=[
            pl.BlockSpec((None, tm, tn), lambda i, j: (i // nst, i % nst, j)),
            pl.BlockSpec((None, tm, LANES), lambda i, j: (i // nst, i % nst, 0)),
            pl.BlockSpec((None, N_GATES, tm), lambda i, j: (i // nst, 0, i % nst)),
        ],
        out_shape=[
            jax.ShapeDtypeStruct((B, S, NM), BF16),
            jax.ShapeDtypeStruct((B, S, LANES), F32),
            jax.ShapeDtypeStruct((B, N_GATES, S), F32),
        ],
        scratch_shapes=[pltpu.VMEM((tm, D), BF16)],
        compiler_params=_params(2),
        name="in_proj",
    )(x, g, w_main, w_gt, b_t)


def _split2(a):
    hi = a.astype(BF16)
    lo = (a - hi.astype(F32)).astype(BF16)
    return hi, lo


def _mlstm_gates(d, lg_ref, lgt_ref):
    L = CHUNK
    r_i = lax.broadcasted_iota(jnp.int32, (L, L), 0)
    c_i = lax.broadcasted_iota(jnp.int32, (L, L), 1)
    low, up = c_i <= r_i, c_i >= r_i
    mask = up if d else low
    m_col = mask.astype(BF16)
    m_row = (low if d else up).astype(BF16)
    lg = lg_ref[...]
    lgt = lgt_ref[...]
    lg_hi, lg_lo = _split2(lg)
    a_all = _dot(m_col, lg_hi) + _dot(m_col, lg_lo)
    lgt_hi, lgt_lo = _split2(lgt)
    b_all = _dot(lgt_hi, m_row) + _dot(lgt_lo, m_row)
    return mask, lgt, a_all, b_all


def _mlstm_scores(d, h, gates, p_ref, c_ref, m_ref, hd):
    mask, lgt, a_all, b_all = gates
    idx = d * MLSTM_HEADS + h
    icol = 2 * d * MLSTM_HEADS + h
    fcol = icol + MLSTM_HEADS
    inv_sqrt = hd ** -0.5
    a_col = a_all[:, fcol:fcol + 1]
    r_row = lgt[icol:icol + 1, :] - b_all[fcol:fcol + 1, :]
    m_prev = m_ref[idx][0:1, 0:1]
    dmat = jnp.where(mask, a_col + r_row, NEG_BIG)
    inter = a_col + m_prev
    m_t = jnp.maximum(inter, jnp.max(dmat, axis=1, keepdims=True))
    w_intra = jnp.exp(dmat - m_t) * inv_sqrt
    w_inter = jnp.exp(inter - m_t) * inv_sqrt
    qb = p_ref[:, h * hd:(h + 1) * hd]
    kb = p_ref[:, (MLSTM_HEADS + h) * hd:(MLSTM_HEADS + h + 1) * hd]
    c_old = c_ref[idx]
    b_tot = jnp.sum(lgt[fcol:fcol + 1, :], axis=1, keepdims=True)
    g_row = b_tot + r_row
    m_new = jnp.maximum(b_tot + m_prev, jnp.max(g_row, axis=1, keepdims=True))
    decay = jnp.exp(b_tot + m_prev - m_new)
    ktw = (kb.astype(F32).T * jnp.exp(g_row - m_new)).astype(BF16)
    return dict(idx=idx, h=h, m_t=m_t, w_intra=w_intra, w_inter=w_inter, c_old=c_old,
                m_new=m_new, decay=decay, ktw=ktw,
                qk=_dot_nt(qb, kb), qc=_dot(qb, c_old.astype(BF16)))


def _mlstm_output(st, p_ref, out_ref, c_ref, m_ref, hd):
    L = CHUNK
    h, idx = st["h"], st["idx"]
    vb = p_ref[:, (2 * MLSTM_HEADS + h) * hd:(2 * MLSTM_HEADS + h + 1) * hd]
    v_aug = jnp.concatenate([vb, jnp.ones((L, LANES), BF16)], axis=1)
    s = st["qk"] * st["w_intra"]
    num = _dot(s.astype(BF16), v_aug) + st["w_inter"] * st["qc"]
    denom = jnp.maximum(jnp.abs(num[:, hd:hd + 1]), jnp.exp(-st["m_t"]))
    out_ref[:, h * hd:(h + 1) * hd] = (num[:, 0:hd] * (1.0 / denom)).astype(out_ref.dtype)

    c_ref[idx] = st["decay"] * st["c_old"] + _dot(st["ktw"], v_aug)
    m_ref[idx] = jnp.broadcast_to(st["m_new"], (8, LANES))


def _mlstm_kernel(pf_ref, pb_ref, lgf_ref, lgb_ref, lgtf_ref, lgtb_ref, hf_ref, hb_ref,
                  c_ref, m_ref, *, hd):
    @pl.when(pl.program_id(1) == 0)
    def _():
        c_ref[...] = jnp.zeros_like(c_ref)
        m_ref[...] = jnp.zeros_like(m_ref)

    p_refs, out_refs = (pf_ref, pb_ref), (hf_ref, hb_ref)
    gates = (_mlstm_gates(0, lgf_ref, lgtf_ref), _mlstm_gates(1, lgb_ref, lgtb_ref))
    items = [(d, h) for h in range(MLSTM_HEADS) for d in range(2)]
    first = lambda d, h: _mlstm_scores(d, h, gates[d], p_refs[d], c_ref, m_ref, hd)
    pending = [first(*it) for it in items[:MLSTM_LOOKAHEAD]]
    for i, (d, h) in enumerate(items):
        st = pending.pop(0)
        if i + MLSTM_LOOKAHEAD < len(items):
            pending.append(first(*items[i + MLSTM_LOOKAHEAD]))
        _mlstm_output(st, p_refs[d], out_refs[d], c_ref, m_ref, hd)


def _mlstm(proj, lg, lgt, width):
    B, S, _ = proj.shape
    nc = S // CHUNK
    hd = width // MLSTM_HEADS
    fwd = lambda b, c: (b, c, 0)
    bwd = lambda b, c: (b, nc - 1 - c, 0)
    return pl.pallas_call(
        functools.partial(_mlstm_kernel, hd=hd),
        grid=(B, nc),
        in_specs=[
            pl.BlockSpec((None, CHUNK, 3 * width), fwd),
            pl.BlockSpec((None, CHUNK, 3 * width), bwd),
            pl.BlockSpec((None, CHUNK, LANES), fwd),
            pl.BlockSpec((None, CHUNK, LANES), bwd),
            pl.BlockSpec((None, N_GATES, CHUNK), lambda b, c: (b, 0, c)),
            pl.BlockSpec((None, N_GATES, CHUNK), lambda b, c: (b, 0, nc - 1 - c)),
        ],
        out_specs=[
            pl.BlockSpec((None, CHUNK, width), fwd),
            pl.BlockSpec((None, CHUNK, width), bwd),
        ],
        out_shape=[jax.ShapeDtypeStruct((B, S, width), BF16)] * 2,
        scratch_shapes=[
            pltpu.VMEM((2 * MLSTM_HEADS, hd, hd + LANES), F32),
            pltpu.VMEM((2 * MLSTM_HEADS, 8, LANES), F32),
        ],
        compiler_params=_params(2),
        name="mlstm",
    )(proj, proj, lg, lg, lgt, lgt)


def _attn_kernel(q_ref, k_ref, v_ref, lq1_ref, lk1_ref, lq2_ref, lk2_ref, g_ref, o_ref,
                 ka_ref, vt_ref, qat_ref, acc_ref, m_ref, r_ref, *, seq, tq, tk, tiles, lambda_init):
    head = pl.program_id(1)
    qi = pl.program_id(2)
    nk = seq // tk
    hd = LANES // 2
    slope = pltpu.bitcast(jnp.full((8, LANES), (126 - head) << 23, jnp.int32), F32)[0:1, 0:1]
    c_f32 = slope * LOG2E

    @pl.when((pl.program_id(0) == 0) & (head == 0) & (qi == 0))
    def _build_constants():
        def body(j, carry):
            pos = j * tk + lax.broadcasted_iota(jnp.int32, (tk, LANES), 0)
            lane = lax.broadcasted_iota(jnp.int32, (tk, LANES), 1)
            s_lo = pos & (LANES - 1)
            feat = jnp.where(lane < 3, s_lo, jnp.where(lane < N_FEAT, pos - s_lo, 0))
            ka_ref[j, :, LANES:2 * LANES] = feat.astype(F32).astype(BF16)
            ones_row = lax.broadcasted_iota(jnp.int32, (ATTN_VROWS - LANES, tk), 0) == 0
            vt_ref[j, LANES:ATTN_VROWS, :] = jnp.where(ones_row, 1.0, 0.0).astype(BF16)
            return carry
        lax.fori_loop(0, nk, body, 0)

    @pl.when(qi == 0)
    def _build_keys():
        def body(j, carry):
            start = pl.multiple_of(j * tk, tk)
            ka_ref[j, :, 0:LANES] = k_ref[pl.ds(start, tk), :]
            vt_ref[j, 0:LANES, :] = v_ref[pl.ds(start, tk), :].astype(F32).T.astype(BF16)
            return carry
        lax.fori_loop(0, nk, body, 0)

    lam = (jnp.exp(jnp.sum(lq1_ref[...] * lk1_ref[...], axis=1, keepdims=True))
           - jnp.exp(jnp.sum(lq2_ref[...] * lk2_ref[...], axis=1, keepdims=True)) + lambda_init)
    row = lax.broadcasted_iota(jnp.int32, (LANES, tq), 0)
    part = jnp.zeros((LANES, tq), F32)
    for r in range(N_FEAT):
        part = jnp.where(row == r, LOG2E_PARTS[r % 3], part)
    feat_q = part * slope
    feats = (feat_q.astype(BF16), (-feat_q).astype(BF16), jnp.zeros((LANES, tq), BF16))

    q0s = [(qi * tiles + t) * tq for t in range(tiles)]
    jds = [q0 // tk for q0 in q0s]

    for t in range(tiles):
        q_t = q_ref[t * tq:(t + 1) * tq, :].astype(F32).T * (hd ** -0.5 * LOG2E)
        q_maps = (jnp.where(row < hd, q_t, 0.0).astype(BF16), jnp.where(row >= hd, q_t, 0.0).astype(BF16))
        for mp in range(2):
            for side in range(3):
                qat_ref[6 * t + 3 * mp + side, 0:LANES, :] = q_maps[mp]
                qat_ref[6 * t + 3 * mp + side, LANES:2 * LANES, :] = feats[side]
        ct = c_f32 * (q0s[t] + lax.broadcasted_iota(jnp.int32, (1, tq), 1)).astype(F32)
        r_ref[3 * t] = -ct
        r_ref[3 * t + 1] = ct
        r_ref[3 * t + 2] = jnp.zeros_like(ct)

    def block(t, idx):
        if idx == 0:
            return jds[t], 2
        j = idx - 1 + (idx - 1 >= jds[t]).astype(jnp.int32)
        return j, (j > jds[t]).astype(jnp.int32)

    def col_max(s_t):
        run = [s_t[8 * i:8 * i + 8, :] for i in range(4)]
        for i in range(4, tk // 8):
            run[i % 4] = jnp.maximum(run[i % 4], s_t[8 * i:8 * i + 8, :])
        top = jnp.maximum(jnp.maximum(run[0], run[1]), jnp.maximum(run[2], run[3]))
        return jnp.max(top, axis=0, keepdims=True)

    def diag_bias(t, j):
        s_pos = j * tk + lax.broadcasted_iota(jnp.int32, (tk, tq), 0)
        t_lane = q0s[t] + lax.broadcasted_iota(jnp.int32, (tk, tq), 1)
        return jnp.abs(s_pos - t_lane).astype(F32)

    def finalize(t):
        outs = []
        for mp in range(2):
            a = acc_ref[2 * t + mp]
            outs.append(a[0:LANES, :] * (1.0 / a[LANES:LANES + 1, :]))
        o_t = outs[0] - lam * outs[1]
        ms = jnp.mean(o_t * o_t, axis=0, keepdims=True)
        y = o_t * lax.rsqrt(ms + EPS) * (g_ref[...] * (1.0 - lambda_init))
        o_ref[t * tq:(t + 1) * tq, :] = y.T.astype(BF16)

    stream = [(t, idx, mp) for t in range(tiles) for idx in range(nk) for mp in range(2)]
    ref = [[None, None] for _ in range(tiles)]
    seen = [[[], []] for _ in range(tiles)]
    excess = [jnp.zeros((1, tq), F32) for _ in range(tiles)]

    def scores(unit):
        t, idx, mp = stream[unit]
        j, side = block(t, idx)
        s_t = _dot(ka_ref[j], qat_ref[6 * t + 3 * mp + side])
        return s_t - c_f32 * diag_bias(t, j) if idx == 0 else s_t

    pending = [scores(u) for u in range(min(ATTN_LOOKAHEAD, len(stream)))]
    for unit, (t, idx, mp) in enumerate(stream):
        j, side = block(t, idx)
        vt, r = vt_ref[j], r_ref[3 * t + side]
        s_t = pending.pop(0)
        if unit + ATTN_LOOKAHEAD < len(stream):
            pending.append(scores(unit + ATTN_LOOKAHEAD))
        bm = col_max(s_t) + r
        if idx == 0:
            new_ref = bm
            acc_ref[2 * t + mp] = _dot(vt, jnp.exp2(s_t - (new_ref - r)).astype(BF16))
        else:
            old_ref = ref[t][mp]
            new_ref = jnp.maximum(old_ref, seen[t][mp][max(idx - 2, 0)])
            p = jnp.exp2(s_t - (new_ref - r)).astype(BF16)
            acc_ref[2 * t + mp] = jnp.exp2(old_ref - new_ref) * acc_ref[2 * t + mp] + _dot(vt, p)
            excess[t] = jnp.maximum(excess[t], bm - new_ref)
        ref[t][mp] = new_ref
        seen[t][mp].append(bm)
        if idx == nk - 1 and mp == 1:
            finalize(t)

    for t in range(tiles):
        @pl.when(jnp.max(excess[t]) > ATTN_MAX_EXCESS)
        def _exact_tile(t=t):
            m_ref[...] = jnp.full(m_ref.shape, NEG_BIG, F32)
            for mp in range(2):
                acc_ref[2 * t + mp] = jnp.zeros((ATTN_VROWS, tq), F32)

            def body(j, carry):
                side = jnp.where(j < jds[t], 0, jnp.where(j == jds[t], 2, 1))
                ka, vt, r = ka_ref[j], vt_ref[j], r_ref[3 * t + side]
                bias = jnp.where(j == jds[t], c_f32, 0.0) * diag_bias(t, j)
                for mp in range(2):
                    s_t = _dot(ka, qat_ref[6 * t + 3 * mp + side]) - bias
                    m_old = m_ref[mp]
                    m_new = jnp.maximum(m_old, col_max(s_t) + r)
                    p = jnp.exp2(s_t - (m_new - r)).astype(BF16)
                    acc_ref[2 * t + mp] = jnp.exp2(m_old - m_new) * acc_ref[2 * t + mp] + _dot(vt, p)
                    m_ref[mp] = m_new
                return carry
            lax.fori_loop(0, nk, body, 0)
            finalize(t)


def _attn(proj, lq1, lk1, lq2, lk2, g_col, col0, lambda_init):
    B, S, _ = proj.shape
    tq, tk = min(ATTN_TQ, S), min(ATTN_TK, S)
    nk = S // tk
    tiles = max(1, min(ATTN_STREAM_BLOCKS // nk, S // tq))
    cb = col0 // LANES
    vec = pl.BlockSpec((1, LANES // 2), lambda b, h, i: (0, 0))
    return pl.pallas_call(
        functools.partial(_attn_kernel, seq=S, tq=tq, tk=tk, tiles=tiles, lambda_init=lambda_init),
        grid=(B, DIFF_HEADS, S // (tiles * tq)),
        in_specs=[
            pl.BlockSpec((None, tiles * tq, LANES), lambda b, h, i: (b, i, cb + h)),
            pl.BlockSpec((None, S, LANES), lambda b, h, i: (b, 0, cb + DIFF_HEADS + h)),
            pl.BlockSpec((None, S, LANES), lambda b, h, i: (b, 0, cb + 2 * DIFF_HEADS + h)),
            vec, vec, vec, vec,
            pl.BlockSpec((LANES, 1), lambda b, h, i: (0, 0)),
        ],
        out_specs=pl.BlockSpec((None, tiles * tq, LANES), lambda b, h, i: (b, i, h)),
        out_shape=jax.ShapeDtypeStruct((B, S, DIFF_HEADS * LANES), BF16),
        scratch_shapes=[
            pltpu.VMEM((nk, tk, 2 * LANES), BF16),
            pltpu.VMEM((nk, ATTN_VROWS, tk), BF16),
            pltpu.VMEM((6 * tiles, 2 * LANES, tq), BF16),
            pltpu.VMEM((2 * tiles, ATTN_VROWS, tq), F32),
            pltpu.VMEM((2, 1, tq), F32),
            pltpu.VMEM((3 * tiles, 1, tq), F32),
        ],
        compiler_params=_params(3),
        name="attn",
    )(proj, proj, proj, lq1, lk1, lq2, lk2, g_col)


def _outproj_kernel(hf_ref, hb_ref, mo_ref, oa_ref, x_ref, gm_ref, w_ref, gp_ref, out_ref, *, hd):
    width = MLSTM_HEADS * hd
    ys = []
    for h in range(MLSTM_HEADS):
        sl = slice(h * hd, (h + 1) * hd)
        hm = _rms(hf_ref[:, sl].astype(F32) + hb_ref[:, sl].astype(F32), gm_ref[:, sl])
        ys.append((jax.nn.sigmoid(mo_ref[:, sl].astype(F32)) * hm).astype(BF16))
    ym = jnp.concatenate(ys, axis=1)
    mix = _dot(ym, w_ref[0:width, :]) + _dot(oa_ref[...], w_ref[width:, :])
    out_ref[...] = x_ref[...] + _rms(mix, gp_ref[...])


def _out_proj(hf, hb, proj, oa, x, gm, w_out, gp, width):
    B, S, D = x.shape
    tm = min(512, S)
    row = lambda b, i: (b, i, 0)
    const = lambda b, i: (0, 0)
    return pl.pallas_call(
        functools.partial(_outproj_kernel, hd=width // MLSTM_HEADS),
        grid=(B, S // tm),
        in_specs=[
            pl.BlockSpec((None, tm, width), row),
            pl.BlockSpec((None, tm, width), row),
            pl.BlockSpec((None, tm, width), lambda b, i: (b, i, 3)),
            pl.BlockSpec((None, tm, oa.shape[2]), row),
            pl.BlockSpec((None, tm, D), row),
            pl.BlockSpec((1, width), const),
            pl.BlockSpec(w_out.shape, const),
            pl.BlockSpec((1, D), const),
        ],
        out_specs=pl.BlockSpec((None, tm, D), row),
        out_shape=jax.ShapeDtypeStruct((B, S, D), F32),
        compiler_params=_params(2),
        name="out_proj",
    )(hf, hb, proj, oa, x, gm, w_out, gp)


def _mlp_kernel(x_ref, gpre_ref, wup_ref, wdown_ref, gpost_ref, out_ref, h_ref, acc_ref):
    f = pl.program_id(2)

    @pl.when(f == 0)
    def _():
        h_ref[...] = _rms(x_ref[...], gpre_ref[...]).astype(BF16)
        acc_ref[...] = jnp.zeros_like(acc_ref)

    u = jnp.maximum(_dot(h_ref[...], wup_ref[...]), 0.0)
    acc_ref[...] += _dot((u * u).astype(BF16), wdown_ref[...])

    @pl.when(f == pl.num_programs(2) - 1)
    def _():
        out_ref[...] = x_ref[...] + _rms(acc_ref[...], gpost_ref[...])


def _mlp(x, gpre, w_up, w_down, gpost):
    B, S, D = x.shape
    F = w_up.shape[1]
    tm = min(512, S)
    tf = 1024
    row = lambda b, i, f: (b, i, 0)
    const = lambda b, i, f: (0, 0)
    return pl.pallas_call(
        _mlp_kernel,
        grid=(B, S // tm, F // tf),
        in_specs=[
            pl.BlockSpec((None, tm, D), row),
            pl.BlockSpec((1, D), const),
            pl.BlockSpec((D, tf), lambda b, i, f: (0, f)),
            pl.BlockSpec((tf, D), lambda b, i, f: (f, 0)),
            pl.BlockSpec((1, D), const),
        ],
        out_specs=pl.BlockSpec((None, tm, D), row),
        out_shape=jax.ShapeDtypeStruct((B, S, D), F32),
        scratch_shapes=[pltpu.VMEM((tm, D), BF16), pltpu.VMEM((tm, D), F32)],
        compiler_params=_params(3),
        name="mlp",
    )(x, gpre, w_up, w_down, gpost)


def _layer(x, lambda_init, g_pre_mix, w_in, b_gates, mlstm_norm_g, lq1, lk1, lq2, lk2, subln_g,
           w_out, g_post_mix, g_pre_mlp, w_up, w_down, g_post_mlp):
    D = x.shape[-1]
    width = D // 2
    gate0 = 4 * width
    w_main = jnp.concatenate([w_in[:, :gate0], w_in[:, gate0 + N_GATES:]], axis=1).astype(BF16)
    w_gt = w_in[:, gate0:gate0 + N_GATES].T.astype(BF16)
    b_t = b_gates.reshape(N_GATES, 1)
    row = lambda v: v.reshape(1, -1)

    proj, lg, lgt = _in_proj(x, row(g_pre_mix), w_main, w_gt, b_t)
    hf, hb = _mlstm(proj, lg, lgt, width)
    oa = _attn(proj, row(lq1), row(lk1), row(lq2), row(lk2), subln_g.reshape(-1, 1), gate0, lambda_init)
    x1 = _out_proj(hf, hb, proj, oa, x, row(mlstm_norm_g), w_out.astype(BF16), row(g_post_mix), width)
    return _mlp(x1, row(g_pre_mlp), w_up.astype(BF16), w_down.astype(BF16), row(g_post_mlp))


def _trunk(x, *weights):
    depth = weights[0].shape[0]
    for l in range(depth):
        lambda_init = 0.8 - 0.6 * math.exp(-0.3 * l)
        x = _layer(x, lambda_init, *[w[l] for w in weights])
    return x


def kernel(x_prompt, x_sample, g_pre_mix, w_in, b_gates, mlstm_norm_g, lambda_q1, lambda_k1, lambda_q2,
           lambda_k2, subln_g, w_out, g_post_mix, g_pre_mlp, w_up, w_down, g_post_mlp):
    weights = (g_pre_mix, w_in, b_gates, mlstm_norm_g, lambda_q1, lambda_k1, lambda_q2, lambda_k2,
               subln_g, w_out, g_post_mix, g_pre_mlp, w_up, w_down, g_post_mlp)
    return (_trunk(x_prompt, *weights), _trunk(x_sample, *weights))
```

```python
import functools
import math
import struct

import jax
import jax.numpy as jnp
from jax import lax
from jax.experimental import pallas as pl
from jax.experimental.pallas import tpu as pltpu

F32 = jnp.float32
BF16 = jnp.bfloat16

EPS = 1e-6
MLSTM_HEADS = 4
DIFF_HEADS = 8
CHUNK = 256
MLSTM_ROWS = 1
MLSTM_LOOKAHEAD = 2
N_GATES = 4 * MLSTM_HEADS
LANES = 128
NEG_BIG = -1e30
VMEM_LIMIT = 56 * 1024 * 1024

ATTN_TQ = 256
ATTN_TK = 256
ATTN_STREAM_BLOCKS = 128
ATTN_VROWS = 144
N_FEAT = 6
ATTN_LOOKAHEAD = 4
ATTN_MAX_EXCESS = 64.0


def _f32_bits(x):
    return struct.unpack("<I", struct.pack("<f", x))[0]


def _round_f32(x):
    return struct.unpack("<f", struct.pack("<f", x))[0]


def _round_bf16(x):
    bits = _f32_bits(x)
    bits = (bits + 0x7FFF + ((bits >> 16) & 1)) & 0xFFFF0000
    return struct.unpack("<f", struct.pack("<I", bits))[0]


def _split3(value):
    parts, rest = [], value
    for _ in range(3):
        parts.append(_round_bf16(rest))
        rest = _round_f32(rest - parts[-1])
    assert rest == 0.0
    return parts


LOG2E = _round_f32(math.log2(math.e))
LOG2E_PARTS = _split3(LOG2E)


def _rms(x, g):
    return x * lax.rsqrt(jnp.mean(x * x, axis=-1, keepdims=True) + EPS) * g


def _log_sigmoid(x):
    return jnp.minimum(x, 0.0) - jnp.log1p(jnp.exp(-jnp.abs(x)))


def _dot(a, b):
    return jnp.dot(a, b, preferred_element_type=F32)


def _dot_nt(a, b):
    return lax.dot_general(a, b, (((1,), (1,)), ((), ())), preferred_element_type=F32)


def _params(n_axes, flags=None):
    return pltpu.CompilerParams(dimension_semantics=("arbitrary",) * n_axes,
                                vmem_limit_bytes=VMEM_LIMIT, flags=flags)


def _inproj_kernel(x_ref, g_ref, w_ref, wgt_ref, bt_ref, proj_ref, lg_ref, lgt_ref, hn_ref):
    @pl.when(pl.program_id(1) == 0)
    def _():
        hn = _rms(x_ref[...], g_ref[...]).astype(BF16)
        hn_ref[...] = hn
        gt = _dot_nt(wgt_ref[...], hn) + bt_ref[...]
        row = lax.broadcasted_iota(jnp.int32, gt.shape, 0)
        lgt = jnp.where((row & MLSTM_HEADS) != 0, _log_sigmoid(gt), gt)
        lgt_ref[...] = lgt
        pad = jnp.zeros((LANES - N_GATES, gt.shape[1]), F32)
        lg_ref[...] = jnp.concatenate([lgt, pad], axis=0).T

    proj_ref[...] = _dot(hn_ref[...], w_ref[...]).astype(BF16)


def _in_proj(x, g, w_main, w_gt, b_t):
    B, S, D = x.shape
    NM = w_main.shape[1]
    tm = min(1024, S)
    tn = NM // 4
    nst = S // tm
    return pl.pallas_call(
        _inproj_kernel,
        grid=(B * nst, NM // tn),
        in_specs=[
            pl.BlockSpec((None, tm, D), lambda i, j: (i // nst, i % nst, 0)),
            pl.BlockSpec((1, D), lambda i, j: (0, 0)),
            pl.BlockSpec((D, tn), lambda i, j: (0, j)),
            pl.BlockSpec((N_GATES, D), lambda i, j: (0, 0)),
            pl.BlockSpec((N_GATES, 1), lambda i, j: (0, 0)),
        ],
        out_specs=[
            pl.BlockSpec((None, tm, tn), lambda i, j: (i // nst, i % nst, j)),
            pl.BlockSpec((None, tm, LANES), lambda i, j: (i // nst, i % nst, 0)),
            pl.BlockSpec((None, N_GATES, tm), lambda i, j: (i // nst, 0, i % nst)),
        ],
        out_shape=[
            jax.ShapeDtypeStruct((B, S, NM), BF16),
            jax.ShapeDtypeStruct((B, S, LANES), F32),
            jax.ShapeDtypeStruct((B, N_GATES, S), F32),
        ],
        scratch_shapes=[pltpu.VMEM((tm, D), BF16)],
        compiler_params=_params(2),
        name="in_proj",
    )(x, g, w_main, w_gt, b_t)


def _split2(a):
    hi = a.astype(BF16)
    lo = (a - hi.astype(F32)).astype(BF16)
    return hi, lo


def _mlstm_gates(d, lg_ref, lgt_ref):
    L = CHUNK
    r_i = lax.broadcasted_iota(jnp.int32, (L, L), 0)
    c_i = lax.broadcasted_iota(jnp.int32, (L, L), 1)
    low, up = c_i <= r_i, c_i >= r_i
    mask = up if d else low
    m_col = mask.astype(BF16)
    m_row = (low if d else up).astype(BF16)
    lg = lg_ref[...]
    lgt = lgt_ref[...]
    lg_hi, lg_lo = _split2(lg)
    a_all = _dot(m_col, lg_hi) + _dot(m_col, lg_lo)
    lgt_hi, lgt_lo = _split2(lgt)
    b_all = _dot(lgt_hi, m_row) + _dot(lgt_lo, m_row)
    return mask, lgt, a_all, b_all


def _mlstm_scores(row, d, h, gates, p_ref, c_ref, m_ref, hd):
    mask, lgt, a_all, b_all = gates
    idx = (2 * row + d) * MLSTM_HEADS + h
    icol = 2 * d * MLSTM_HEADS + h
    fcol = icol + MLSTM_HEADS
    inv_sqrt = hd ** -0.5
    a_col = a_all[:, fcol:fcol + 1]
    r_row = lgt[icol:icol + 1, :] - b_all[fcol:fcol + 1, :]
    m_prev = m_ref[idx][0:1, 0:1]
    dmat = jnp.where(mask, a_col + r_row, NEG_BIG)
    inter = a_col + m_prev
    m_t = jnp.maximum(inter, jnp.max(dmat, axis=1, keepdims=True))
    w_intra = jnp.exp(dmat - m_t) * inv_sqrt
    w_inter = jnp.exp(inter - m_t) * inv_sqrt
    qb = p_ref[:, h * hd:(h + 1) * hd]
    kb = p_ref[:, (MLSTM_HEADS + h) * hd:(MLSTM_HEADS + h + 1) * hd]
    c_old = c_ref[idx]
    b_tot = jnp.sum(lgt[fcol:fcol + 1, :], axis=1, keepdims=True)
    g_row = b_tot + r_row
    m_new = jnp.maximum(b_tot + m_prev, jnp.max(g_row, axis=1, keepdims=True))
    decay = jnp.exp(b_tot + m_prev - m_new)
    ktw = (kb.astype(F32).T * jnp.exp(g_row - m_new)).astype(BF16)
    return dict(idx=idx, h=h, m_t=m_t, w_intra=w_intra, w_inter=w_inter, c_old=c_old,
                m_new=m_new, decay=decay, ktw=ktw,
                qk=_dot_nt(qb, kb), qc=_dot(qb, c_old.astype(BF16)))


def _mlstm_output(st, p_ref, out_ref, c_ref, m_ref, hd):
    L = CHUNK
    h, idx = st["h"], st["idx"]
    vb = p_ref[:, (2 * MLSTM_HEADS + h) * hd:(2 * MLSTM_HEADS + h + 1) * hd]
    v_aug = jnp.concatenate([vb, jnp.ones((L, LANES), BF16)], axis=1)
    s = st["qk"] * st["w_intra"]
    num = _dot(s.astype(BF16), v_aug) + st["w_inter"] * st["qc"]
    denom = jnp.maximum(jnp.abs(num[:, hd:hd + 1]), jnp.exp(-st["m_t"]))
    out_ref[:, h * hd:(h + 1) * hd] = (num[:, 0:hd] * (1.0 / denom)).astype(out_ref.dtype)

    c_ref[idx] = st["decay"] * st["c_old"] + _dot(st["ktw"], v_aug)
    m_ref[idx] = jnp.broadcast_to(st["m_new"], (8, LANES))


def _mlstm_kernel(pf_ref, pb_ref, lgf_ref, lgb_ref, lgtf_ref, lgtb_ref, hf_ref, hb_ref,
                  c_ref, m_ref, *, hd):
    @pl.when(pl.program_id(1) == 0)
    def _():
        c_ref[...] = jnp.zeros_like(c_ref)
        m_ref[...] = jnp.zeros_like(m_ref)

    rows = pf_ref.shape[0]
    p_refs, out_refs = (pf_ref, pb_ref), (hf_ref, hb_ref)
    lg_refs, lgt_refs = (lgf_ref, lgb_ref), (lgtf_ref, lgtb_ref)
    gates = [[_mlstm_gates(d, lg_refs[d].at[r], lgt_refs[d].at[r]) for d in range(2)] for r in range(rows)]
    items = [(r, d, h) for h in range(MLSTM_HEADS) for d in range(2) for r in range(rows)]
    first = lambda r, d, h: _mlstm_scores(r, d, h, gates[r][d], p_refs[d].at[r], c_ref, m_ref, hd)
    pending = [first(*it) for it in items[:MLSTM_LOOKAHEAD]]
    for i, (r, d, h) in enumerate(items):
        st = pending.pop(0)
        if i + MLSTM_LOOKAHEAD < len(items):
            pending.append(first(*items[i + MLSTM_LOOKAHEAD]))
        _mlstm_output(st, p_refs[d].at[r], out_refs[d].at[r], c_ref, m_ref, hd)


def _mlstm(proj, lg, lgt, width):
    B, S, _ = proj.shape
    nc = S // CHUNK
    hd = width // MLSTM_HEADS
    rows = MLSTM_ROWS if B % MLSTM_ROWS == 0 else 1
    fwd = lambda b, c: (b, c, 0)
    bwd = lambda b, c: (b, nc - 1 - c, 0)
    return pl.pallas_call(
        functools.partial(_mlstm_kernel, hd=hd),
        grid=(B // rows, nc),
        in_specs=[
            pl.BlockSpec((rows, CHUNK, 3 * width), fwd),
            pl.BlockSpec((rows, CHUNK, 3 * width), bwd),
            pl.BlockSpec((rows, CHUNK, LANES), fwd),
            pl.BlockSpec((rows, CHUNK, LANES), bwd),
            pl.BlockSpec((rows, N_GATES, CHUNK), lambda b, c: (b, 0, c)),
            pl.BlockSpec((rows, N_GATES, CHUNK), lambda b, c: (b, 0, nc - 1 - c)),
        ],
        out_specs=[
            pl.BlockSpec((rows, CHUNK, width), fwd),
            pl.BlockSpec((rows, CHUNK, width), bwd),
        ],
        out_shape=[jax.ShapeDtypeStruct((B, S, width), BF16)] * 2,
        scratch_shapes=[
            pltpu.VMEM((2 * rows * MLSTM_HEADS, hd, hd + LANES), F32),
            pltpu.VMEM((2 * rows * MLSTM_HEADS, 8, LANES), F32),
        ],
        compiler_params=_params(2),
        name="mlstm",
    )(proj, proj, lg, lg, lgt, lgt)


def _attn_kernel(q_ref, k_ref, v_ref, lq1_ref, lk1_ref, lq2_ref, lk2_ref, g_ref, o_ref,
                 ka_ref, vt_ref, qat_ref, acc_ref, m_ref, r_ref, *, seq, tq, tk, tiles, lambda_init):
    head = pl.program_id(1)
    qi = pl.program_id(2)
    nk = seq // tk
    hd = LANES // 2
    slope = pltpu.bitcast(jnp.full((8, LANES), (126 - head) << 23, jnp.int32), F32)[0:1, 0:1]
    c_f32 = slope * LOG2E

    @pl.when((pl.program_id(0) == 0) & (head == 0) & (qi == 0))
    def _build_constants():
        def body(j, carry):
            pos = j * tk + lax.broadcasted_iota(jnp.int32, (tk, LANES), 0)
            lane = lax.broadcasted_iota(jnp.int32, (tk, LANES), 1)
            s_lo = pos & (LANES - 1)
            feat = jnp.where(lane < 3, s_lo, jnp.where(lane < N_FEAT, pos - s_lo, 0))
            ka_ref[j, :, LANES:2 * LANES] = feat.astype(F32).astype(BF16)
            ones_row = lax.broadcasted_iota(jnp.int32, (ATTN_VROWS - LANES, tk), 0) == 0
            vt_ref[j, LANES:ATTN_VROWS, :] = jnp.where(ones_row, 1.0, 0.0).astype(BF16)
            return carry
        lax.fori_loop(0, nk, body, 0)

    @pl.when(qi == 0)
    def _build_keys():
        def body(j, carry):
            start = pl.multiple_of(j * tk, tk)
            ka_ref[j, :, 0:LANES] = k_ref[pl.ds(start, tk), :]
            vt_ref[j, 0:LANES, :] = v_ref[pl.ds(start, tk), :].astype(F32).T.astype(BF16)
            return carry
        lax.fori_loop(0, nk, body, 0)

    lam = (jnp.exp(jnp.sum(lq1_ref[...] * lk1_ref[...], axis=1, keepdims=True))
           - jnp.exp(jnp.sum(lq2_ref[...] * lk2_ref[...], axis=1, keepdims=True)) + lambda_init)
    row = lax.broadcasted_iota(jnp.int32, (LANES, tq), 0)
    part = jnp.zeros((LANES, tq), F32)
    for r in range(N_FEAT):
        part = jnp.where(row == r, LOG2E_PARTS[r % 3], part)
    feat_q = part * slope
    feats = (feat_q.astype(BF16), (-feat_q).astype(BF16), jnp.zeros((LANES, tq), BF16))

    q0s = [(qi * tiles + t) * tq for t in range(tiles)]
    jds = [q0 // tk for q0 in q0s]

    for t in range(tiles):
        q_t = q_ref[t * tq:(t + 1) * tq, :].astype(F32).T * (hd ** -0.5 * LOG2E)
        q_maps = (jnp.where(row < hd, q_t, 0.0).astype(BF16), jnp.where(row >= hd, q_t, 0.0).astype(BF16))
        for mp in range(2):
            for side in range(3):
                qat_ref[6 * t + 3 * mp + side, 0:LANES, :] = q_maps[mp]
                qat_ref[6 * t + 3 * mp + side, LANES:2 * LANES, :] = feats[side]
        ct = c_f32 * (q0s[t] + lax.broadcasted_iota(jnp.int32, (1, tq), 1)).astype(F32)
        r_ref[3 * t] = -ct
        r_ref[3 * t + 1] = ct
        r_ref[3 * t + 2] = jnp.zeros_like(ct)

    def block(t, idx):
        if idx == 0:
            return jds[t], 2
        j = idx - 1 + (idx - 1 >= jds[t]).astype(jnp.int32)
        return j, (j > jds[t]).astype(jnp.int32)

    def col_max(s_t):
        run = [s_t[8 * i:8 * i + 8, :] for i in range(4)]
        for i in range(4, tk // 8):
            run[i % 4] = jnp.maximum(run[i % 4], s_t[8 * i:8 * i + 8, :])
        top = jnp.maximum(jnp.maximum(run[0], run[1]), jnp.maximum(run[2], run[3]))
        return jnp.max(top, axis=0, keepdims=True)

    def diag_bias(t, j):
        s_pos = j * tk + lax.broadcasted_iota(jnp.int32, (tk, tq), 0)
        t_lane = q0s[t] + lax.broadcasted_iota(jnp.int32, (tk, tq), 1)
        return jnp.abs(s_pos - t_lane).astype(F32)

    def finalize(t):
        outs = []
        for mp in range(2):
            a = acc_ref[2 * t + mp]
            outs.append(a[0:LANES, :] * (1.0 / a[LANES:LANES + 1, :]))
        o_t = outs[0] - lam * outs[1]
        ms = jnp.mean(o_t * o_t, axis=0, keepdims=True)
        y = o_t * lax.rsqrt(ms + EPS) * (g_ref[...] * (1.0 - lambda_init))
        o_ref[t * tq:(t + 1) * tq, :] = y.T.astype(BF16)

    stream = [(t, idx, mp) for t in range(tiles) for idx in range(nk) for mp in range(2)]
    ref = [[None, None] for _ in range(tiles)]
    seen = [[[], []] for _ in range(tiles)]
    excess = [jnp.zeros((1, tq), F32) for _ in range(tiles)]

    def scores(unit):
        t, idx, mp = stream[unit]
        j, side = block(t, idx)
        s_t = _dot(ka_ref[j], qat_ref[6 * t + 3 * mp + side])
        return s_t - c_f32 * diag_bias(t, j) if idx == 0 else s_t

    pending = [scores(u) for u in range(min(ATTN_LOOKAHEAD, len(stream)))]
    for unit, (t, idx, mp) in enumerate(stream):
        j, side = block(t, idx)
        vt, r = vt_ref[j], r_ref[3 * t + side]
        s_t = pending.pop(0)
        if unit + ATTN_LOOKAHEAD < len(stream):
            pending.append(scores(unit + ATTN_LOOKAHEAD))
        bm = col_max(s_t) + r
        if idx == 0:
            new_ref = bm
            acc_ref[2 * t + mp] = _dot(vt, jnp.exp2(s_t - (new_ref - r)).astype(BF16))
        else:
            old_ref = ref[t][mp]
            new_ref = jnp.maximum(old_ref, seen[t][mp][max(idx - 2, 0)])
            p = jnp.exp2(s_t - (new_ref - r)).astype(BF16)
            acc_ref[2 * t + mp] = jnp.exp2(old_ref - new_ref) * acc_ref[2 * t + mp] + _dot(vt, p)
            excess[t] = jnp.maximum(excess[t], bm - new_ref)
        ref[t][mp] = new_ref
        seen[t][mp].append(bm)
        if idx == nk - 1 and mp == 1:
            finalize(t)

    for t in range(tiles):
        @pl.when(jnp.max(excess[t]) > ATTN_MAX_EXCESS)
        def _exact_tile(t=t):
            m_ref[...] = jnp.full(m_ref.shape, NEG_BIG, F32)
            for mp in range(2):
                acc_ref[2 * t + mp] = jnp.zeros((ATTN_VROWS, tq), F32)

            def body(j, carry):
                side = jnp.where(j < jds[t], 0, jnp.where(j == jds[t], 2, 1))
                ka, vt, r = ka_ref[j], vt_ref[j], r_ref[3 * t + side]
                bias = jnp.where(j == jds[t], c_f32, 0.0) * diag_bias(t, j)
                for mp in range(2):
                    s_t = _dot(ka, qat_ref[6 * t + 3 * mp + side]) - bias
                    m_old = m_ref[mp]
                    m_new = jnp.maximum(m_old, col_max(s_t) + r)
                    p = jnp.exp2(s_t - (m_new - r)).astype(BF16)
                    acc_ref[2 * t + mp] = jnp.exp2(m_old - m_new) * acc_ref[2 * t + mp] + _dot(vt, p)
                    m_ref[mp] = m_new
                return carry
            lax.fori_loop(0, nk, body, 0)
            finalize(t)


def _attn(proj, lq1, lk1, lq2, lk2, g_col, col0, lambda_init):
    B, S, _ = proj.shape
    tq, tk = min(ATTN_TQ, S), min(ATTN_TK, S)
    nk = S // tk
    tiles = max(1, min(ATTN_STREAM_BLOCKS // nk, S // tq))
    cb = col0 // LANES
    vec = pl.BlockSpec((1, LANES // 2), lambda b, h, i: (0, 0))
    return pl.pallas_call(
        functools.partial(_attn_kernel, seq=S, tq=tq, tk=tk, tiles=tiles, lambda_init=lambda_init),
        grid=(B, DIFF_HEADS, S // (tiles * tq)),
        in_specs=[
            pl.BlockSpec((None, tiles * tq, LANES), lambda b, h, i: (b, i, cb + h)),
            pl.BlockSpec((None, S, LANES), lambda b, h, i: (b, 0, cb + DIFF_HEADS + h)),
            pl.BlockSpec((None, S, LANES), lambda b, h, i: (b, 0, cb + 2 * DIFF_HEADS + h)),
            vec, vec, vec, vec,
            pl.BlockSpec((LANES, 1), lambda b, h, i: (0, 0)),
        ],
        out_specs=pl.BlockSpec((None, tiles * tq, LANES), lambda b, h, i: (b, i, h)),
        out_shape=jax.ShapeDtypeStruct((B, S, DIFF_HEADS * LANES), BF16),
        scratch_shapes=[
            pltpu.VMEM((nk, tk, 2 * LANES), BF16),
            pltpu.VMEM((nk, ATTN_VROWS, tk), BF16),
            pltpu.VMEM((6 * tiles, 2 * LANES, tq), BF16),
            pltpu.VMEM((2 * tiles, ATTN_VROWS, tq), F32),
            pltpu.VMEM((2, 1, tq), F32),
            pltpu.VMEM((3 * tiles, 1, tq), F32),
        ],
        compiler_params=_params(3),
        name="attn",
    )(proj, proj, proj, lq1, lk1, lq2, lk2, g_col)


def _outproj_kernel(hf_ref, hb_ref, mo_ref, oa_ref, x_ref, gm_ref, w_ref, gp_ref, out_ref, *, hd):
    width = MLSTM_HEADS * hd
    ys = []
    for h in range(MLSTM_HEADS):
        sl = slice(h * hd, (h + 1) * hd)
        hm = _rms(hf_ref[:, sl].astype(F32) + hb_ref[:, sl].astype(F32), gm_ref[:, sl])
        ys.append((jax.nn.sigmoid(mo_ref[:, sl].astype(F32)) * hm).astype(BF16))
    ym = jnp.concatenate(ys, axis=1)
    mix = _dot(ym, w_ref[0:width, :]) + _dot(oa_ref[...], w_ref[width:, :])
    out_ref[...] = x_ref[...] + _rms(mix, gp_ref[...])


def _out_proj(hf, hb, proj, oa, x, gm, w_out, gp, width):
    B, S, D = x.shape
    tm = min(512, S)
    row = lambda b, i: (b, i, 0)
    const = lambda b, i: (0, 0)
    return pl.pallas_call(
        functools.partial(_outproj_kernel, hd=width // MLSTM_HEADS),
        grid=(B, S // tm),
        in_specs=[
            pl.BlockSpec((None, tm, width), row),
            pl.BlockSpec((None, tm, width), row),
            pl.BlockSpec((None, tm, width), lambda b, i: (b, i, 3)),
            pl.BlockSpec((None, tm, oa.shape[2]), row),
            pl.BlockSpec((None, tm, D), row),
            pl.BlockSpec((1, width), const),
            pl.BlockSpec(w_out.shape, const),
            pl.BlockSpec((1, D), const),
        ],
        out_specs=pl.BlockSpec((None, tm, D), row),
        out_shape=jax.ShapeDtypeStruct((B, S, D), F32),
        compiler_params=_params(2),
        name="out_proj",
    )(hf, hb, proj, oa, x, gm, w_out, gp)


def _mlp_kernel(x_ref, gpre_ref, wup_ref, wdown_ref, gpost_ref, out_ref, h_ref, acc_ref):
    f = pl.program_id(2)

    @pl.when(f == 0)
    def _():
        h_ref[...] = _rms(x_ref[...], gpre_ref[...]).astype(BF16)
        acc_ref[...] = jnp.zeros_like(acc_ref)

    u = jnp.maximum(_dot(h_ref[...], wup_ref[...]), 0.0)
    acc_ref[...] += _dot((u * u).astype(BF16), wdown_ref[...])

    @pl.when(f == pl.num_programs(2) - 1)
    def _():
        out_ref[...] = x_ref[...] + _rms(acc_ref[...], gpost_ref[...])


def _mlp(x, gpre, w_up, w_down, gpost):
    B, S, D = x.shape
    F = w_up.shape[1]
    tm = min(512, S)
    tf = 1024
    row = lambda b, i, f: (b, i, 0)
    const = lambda b, i, f: (0, 0)
    return pl.pallas_call(
        _mlp_kernel,
        grid=(B, S // tm, F // tf),
        in_specs=[
            pl.BlockSpec((None, tm, D), row),
            pl.BlockSpec((1, D), const),
            pl.BlockSpec((D, tf), lambda b, i, f: (0, f)),
            pl.BlockSpec((tf, D), lambda b, i, f: (f, 0)),
            pl.BlockSpec((1, D), const),
        ],
        out_specs=pl.BlockSpec((None, tm, D), row),
        out_shape=jax.ShapeDtypeStruct((B, S, D), F32),
        scratch_shapes=[pltpu.VMEM((tm, D), BF16), pltpu.VMEM((tm, D), F32)],
        compiler_params=_params(3),
        name="mlp",
    )(x, gpre, w_up, w_down, gpost)


def _layer(x, lambda_init, g_pre_mix, w_in, b_gates, mlstm_norm_g, lq1, lk1, lq2, lk2, subln_g,
           w_out, g_post_mix, g_pre_mlp, w_up, w_down, g_post_mlp):
    D = x.shape[-1]
    width = D // 2
    gate0 = 4 * width
    w_bf = w_in.astype(BF16)
    w_main = jnp.concatenate([w_bf[:, :gate0], w_bf[:, gate0 + N_GATES:]], axis=1)
    w_gt = w_bf[:, gate0:gate0 + N_GATES].T
    b_t = b_gates.reshape(N_GATES, 1)
    row = lambda v: v.reshape(1, -1)

    proj, lg, lgt = _in_proj(x, row(g_pre_mix), w_main, w_gt, b_t)
    hf, hb = _mlstm(proj, lg, lgt, width)
    oa = _attn(proj, row(lq1), row(lk1), row(lq2), row(lk2), subln_g.reshape(-1, 1), gate0, lambda_init)
    x1 = _out_proj(hf, hb, proj, oa, x, row(mlstm_norm_g), w_out.astype(BF16), row(g_post_mix), width)
    return _mlp(x1, row(g_pre_mlp), w_up.astype(BF16), w_down.astype(BF16), row(g_post_mlp))


def _trunk(x, *weights):
    depth = weights[0].shape[0]
    for l in range(depth):
        lambda_init = 0.8 - 0.6 * math.exp(-0.3 * l)
        x = _layer(x, lambda_init, *[w[l] for w in weights])
    return x


def kernel(x_prompt, x_sample, g_pre_mix, w_in, b_gates, mlstm_norm_g, lambda_q1, lambda_k1, lambda_q2,
           lambda_k2, subln_g, w_out, g_post_mix, g_pre_mlp, w_up, w_down, g_post_mlp):
    weights = (g_pre_mix, w_in, b_gates, mlstm_norm_g, lambda_q1, lambda_k1, lambda_q2, lambda_k2,
               subln_g, w_out, g_post_mix, g_pre_mlp, w_up, w_down, g_post_mlp)
    return (_trunk(x_prompt, *weights), _trunk(x_sample, *weights))
```

```python
import functools
import math
import struct

import jax
import jax.numpy as jnp
from jax import lax
from jax.experimental import pallas as pl
from jax.experimental.pallas import tpu as pltpu

F32 = jnp.float32
BF16 = jnp.bfloat16

EPS = 1e-6
MLSTM_HEADS = 4
DIFF_HEADS = 8
CHUNK = 256
MLSTM_ROWS = 1
MLSTM_LOOKAHEAD = 2
N_GATES = 4 * MLSTM_HEADS
LANES = 128
NEG_BIG = -1e30
VMEM_LIMIT = 56 * 1024 * 1024

ATTN_TQ = 256
ATTN_TK = 256
ATTN_STREAM_BLOCKS = 128
ATTN_VROWS = 144
N_FEAT = 6
ATTN_LOOKAHEAD = 4
ATTN_MAX_EXCESS = 64.0


def _f32_bits(x):
    return struct.unpack("<I", struct.pack("<f", x))[0]


def _round_f32(x):
    return struct.unpack("<f", struct.pack("<f", x))[0]


def _round_bf16(x):
    bits = _f32_bits(x)
    bits = (bits + 0x7FFF + ((bits >> 16) & 1)) & 0xFFFF0000
    return struct.unpack("<f", struct.pack("<I", bits))[0]


def _split3(value):
    parts, rest = [], value
    for _ in range(3):
        parts.append(_round_bf16(rest))
        rest = _round_f32(rest - parts[-1])
    assert rest == 0.0
    return parts


LOG2E = _round_f32(math.log2(math.e))
LOG2E_PARTS = _split3(LOG2E)


def _rms(x, g):
    return x * lax.rsqrt(jnp.mean(x * x, axis=-1, keepdims=True) + EPS) * g


def _log_sigmoid(x):
    return jnp.minimum(x, 0.0) - jnp.log1p(jnp.exp(-jnp.abs(x)))


def _dot(a, b):
    return jnp.dot(a, b, preferred_element_type=F32)


def _dot_nt(a, b):
    return lax.dot_general(a, b, (((1,), (1,)), ((), ())), preferred_element_type=F32)


def _params(n_axes, flags=None):
    return pltpu.CompilerParams(dimension_semantics=("arbitrary",) * n_axes,
                                vmem_limit_bytes=VMEM_LIMIT, flags=flags)


def _inproj_kernel(x_ref, g_ref, w_ref, wgt_ref, bt_ref, proj_ref, lg_ref, lgt_ref, hn_ref):
    @pl.when(pl.program_id(1) == 0)
    def _():
        hn = _rms(x_ref[...], g_ref[...]).astype(BF16)
        hn_ref[...] = hn
        gt = _dot_nt(wgt_ref[...], hn) + bt_ref[...]
        row = lax.broadcasted_iota(jnp.int32, gt.shape, 0)
        lgt = jnp.where((row & MLSTM_HEADS) != 0, _log_sigmoid(gt), gt)
        lgt_ref[...] = lgt
        pad = jnp.zeros((LANES - N_GATES, gt.shape[1]), F32)
        lg_ref[...] = jnp.concatenate([lgt, pad], axis=0).T

    proj_ref[...] = _dot(hn_ref[...], w_ref[...]).astype(BF16)


def _in_proj(x, g, w_main, w_gt, b_t):
    B, S, D = x.shape
    NM = w_main.shape[1]
    tm = min(1024, S)
    tn = NM // 4
    nst = S // tm
    return pl.pallas_call(
        _inproj_kernel,
        grid=(B * nst, NM // tn),
        in_specs=[
            pl.BlockSpec((None, tm, D), lambda i, j: (i // nst, i % nst, 0)),
            pl.BlockSpec((1, D), lambda i, j: (0, 0)),
            pl.BlockSpec((D, tn), lambda i, j: (0, j)),
            pl.BlockSpec((N_GATES, D), lambda i, j: (0, 0)),
            pl.BlockSpec((N_GATES, 1), lambda i, j: (0, 0)),
        ],
        out_specs=[
            pl.BlockSpec((None, tm, tn), lambda i, j: (i // nst, i % nst, j)),
            pl.BlockSpec((None, tm, LANES), lambda i, j: (i // nst, i % nst, 0)),
            pl.BlockSpec((None, N_GATES, tm), lambda i, j: (i // nst, 0, i % nst)),
        ],
        out_shape=[
            jax.ShapeDtypeStruct((B, S, NM), BF16),
            jax.ShapeDtypeStruct((B, S, LANES), F32),
            jax.ShapeDtypeStruct((B, N_GATES, S), F32),
        ],
        scratch_shapes=[pltpu.VMEM((tm, D), BF16)],
        compiler_params=_params(2),
        name="in_proj",
    )(x, g, w_main, w_gt, b_t)


def _split2(a):
    hi = a.astype(BF16)
    lo = (a - hi.astype(F32)).astype(BF16)
    return hi, lo


def _mlstm_gates(d, lg_ref, lgt_ref):
    L = CHUNK
    r_i = lax.broadcasted_iota(jnp.int32, (L, L), 0)
    c_i = lax.broadcasted_iota(jnp.int32, (L, L), 1)
    low, up = c_i <= r_i, c_i >= r_i
    mask = up if d else low
    m_col = mask.astype(BF16)
    m_row = (low if d else up).astype(BF16)
    lg = lg_ref[...]
    lgt = lgt_ref[...]
    lg_hi, lg_lo = _split2(lg)
    a_all = _dot(m_col, lg_hi) + _dot(m_col, lg_lo)
    lgt_hi, lgt_lo = _split2(lgt)
    b_all = _dot(lgt_hi, m_row) + _dot(lgt_lo, m_row)
    return mask, lgt, a_all, b_all


def _mlstm_scores(row, d, h, gates, p_ref, c_ref, m_ref, hd):
    mask, lgt, a_all, b_all = gates
    idx = (2 * row + d) * MLSTM_HEADS + h
    icol = 2 * d * MLSTM_HEADS + h
    fcol = icol + MLSTM_HEADS
    inv_sqrt = hd ** -0.5
    a_col = a_all[:, fcol:fcol + 1]
    r_row = lgt[icol:icol + 1, :] - b_all[fcol:fcol + 1, :]
    m_prev = m_ref[idx][0:1, 0:1]
    dmat = jnp.where(mask, a_col + r_row, NEG_BIG)
    inter = a_col + m_prev
    m_t = jnp.maximum(inter, jnp.max(dmat, axis=1, keepdims=True))
    w_intra = jnp.exp(dmat - m_t) * inv_sqrt
    w_inter = jnp.exp(inter - m_t) * inv_sqrt
    qb = p_ref[:, h * hd:(h + 1) * hd]
    kb = p_ref[:, (MLSTM_HEADS + h) * hd:(MLSTM_HEADS + h + 1) * hd]
    c_old = c_ref[idx]
    b_tot = jnp.sum(lgt[fcol:fcol + 1, :], axis=1, keepdims=True)
    g_row = b_tot + r_row
    m_new = jnp.maximum(b_tot + m_prev, jnp.max(g_row, axis=1, keepdims=True))
    decay = jnp.exp(b_tot + m_prev - m_new)
    ktw = (kb.astype(F32).T * jnp.exp(g_row - m_new)).astype(BF16)
    return dict(idx=idx, h=h, m_t=m_t, w_intra=w_intra, w_inter=w_inter, c_old=c_old,
                m_new=m_new, decay=decay, ktw=ktw,
                qk=_dot_nt(qb, kb), qc=_dot(qb, c_old.astype(BF16)))


def _mlstm_output(st, p_ref, out_ref, c_ref, m_ref, hd):
    L = CHUNK
    h, idx = st["h"], st["idx"]
    vb = p_ref[:, (2 * MLSTM_HEADS + h) * hd:(2 * MLSTM_HEADS + h + 1) * hd]
    v_aug = jnp.concatenate([vb, jnp.ones((L, LANES), BF16)], axis=1)
    s = st["qk"] * st["w_intra"]
    num = _dot(s.astype(BF16), v_aug) + st["w_inter"] * st["qc"]
    denom = jnp.maximum(jnp.abs(num[:, hd:hd + 1]), jnp.exp(-st["m_t"]))
    out_ref[:, h * hd:(h + 1) * hd] = (num[:, 0:hd] * (1.0 / denom)).astype(out_ref.dtype)

    c_ref[idx] = st["decay"] * st["c_old"] + _dot(st["ktw"], v_aug)
    m_ref[idx] = jnp.broadcast_to(st["m_new"], (8, LANES))


def _mlstm_kernel(pf_ref, pb_ref, lgf_ref, lgb_ref, lgtf_ref, lgtb_ref, hf_ref, hb_ref,
                  c_ref, m_ref, *, hd):
    @pl.when(pl.program_id(1) == 0)
    def _():
        c_ref[...] = jnp.zeros_like(c_ref)
        m_ref[...] = jnp.zeros_like(m_ref)

    rows = pf_ref.shape[0]
    p_refs, out_refs = (pf_ref, pb_ref), (hf_ref, hb_ref)
    lg_refs, lgt_refs = (lgf_ref, lgb_ref), (lgtf_ref, lgtb_ref)
    gates = [[_mlstm_gates(d, lg_refs[d].at[r], lgt_refs[d].at[r]) for d in range(2)] for r in range(rows)]
    items = [(r, d, h) for h in range(MLSTM_HEADS) for d in range(2) for r in range(rows)]
    first = lambda r, d, h: _mlstm_scores(r, d, h, gates[r][d], p_refs[d].at[r], c_ref, m_ref, hd)
    pending = [first(*it) for it in items[:MLSTM_LOOKAHEAD]]
    for i, (r, d, h) in enumerate(items):
        st = pending.pop(0)
        if i + MLSTM_LOOKAHEAD < len(items):
            pending.append(first(*items[i + MLSTM_LOOKAHEAD]))
        _mlstm_output(st, p_refs[d].at[r], out_refs[d].at[r], c_ref, m_ref, hd)


def _mlstm(proj, lg, lgt, width):
    B, S, _ = proj.shape
    nc = S // CHUNK
    hd = width // MLSTM_HEADS
    rows = MLSTM_ROWS if B % MLSTM_ROWS == 0 else 1
    fwd = lambda b, c: (b, c, 0)
    bwd = lambda b, c: (b, nc - 1 - c, 0)
    return pl.pallas_call(
        functools.partial(_mlstm_kernel, hd=hd),
        grid=(B // rows, nc),
        in_specs=[
            pl.BlockSpec((rows, CHUNK, 3 * width), fwd),
            pl.BlockSpec((rows, CHUNK, 3 * width), bwd),
            pl.BlockSpec((rows, CHUNK, LANES), fwd),
            pl.BlockSpec((rows, CHUNK, LANES), bwd),
            pl.BlockSpec((rows, N_GATES, CHUNK), lambda b, c: (b, 0, c)),
            pl.BlockSpec((rows, N_GATES, CHUNK), lambda b, c: (b, 0, nc - 1 - c)),
        ],
        out_specs=[
            pl.BlockSpec((rows, CHUNK, width), fwd),
            pl.BlockSpec((rows, CHUNK, width), bwd),
        ],
        out_shape=[jax.ShapeDtypeStruct((B, S, width), BF16)] * 2,
        scratch_shapes=[
            pltpu.VMEM((2 * rows * MLSTM_HEADS, hd, hd + LANES), F32),
            pltpu.VMEM((2 * rows * MLSTM_HEADS, 8, LANES), F32),
        ],
        compiler_params=_params(2),
        name="mlstm",
    )(proj, proj, lg, lg, lgt, lgt)


def _attn_kernel(q_ref, k_ref, v_ref, lq1_ref, lk1_ref, lq2_ref, lk2_ref, g_ref, o_ref,
                 ka_ref, vt_ref, qmap_ref, feat_ref, acc_ref, m_ref, r_ref,
                 *, seq, tq, tk, tiles, lambda_init):
    head = pl.program_id(1)
    qi = pl.program_id(2)
    nk = seq // tk
    hd = LANES // 2
    slope = pltpu.bitcast(jnp.full((8, LANES), (126 - head) << 23, jnp.int32), F32)[0:1, 0:1]
    c_f32 = slope * LOG2E

    @pl.when((pl.program_id(0) == 0) & (head == 0) & (qi == 0))
    def _build_constants():
        def body(j, carry):
            pos = j * tk + lax.broadcasted_iota(jnp.int32, (tk, LANES), 0)
            lane = lax.broadcasted_iota(jnp.int32, (tk, LANES), 1)
            s_lo = pos & (LANES - 1)
            feat = jnp.where(lane < 3, s_lo, jnp.where(lane < N_FEAT, pos - s_lo, 0))
            ka_ref[j, :, LANES:2 * LANES] = feat.astype(F32).astype(BF16)
            ones_row = lax.broadcasted_iota(jnp.int32, (ATTN_VROWS - LANES, tk), 0) == 0
            vt_ref[j, LANES:ATTN_VROWS, :] = jnp.where(ones_row, 1.0, 0.0).astype(BF16)
            return carry
        lax.fori_loop(0, nk, body, 0)

    @pl.when(qi == 0)
    def _build_keys():
        def body(j, carry):
            start = pl.multiple_of(j * tk, tk)
            ka_ref[j, :, 0:LANES] = k_ref[pl.ds(start, tk), :]
            vt_ref[j, 0:LANES, :] = v_ref[pl.ds(start, tk), :].astype(F32).T.astype(BF16)
            return carry
        lax.fori_loop(0, nk, body, 0)

    lam = (jnp.exp(jnp.sum(lq1_ref[...] * lk1_ref[...], axis=1, keepdims=True))
           - jnp.exp(jnp.sum(lq2_ref[...] * lk2_ref[...], axis=1, keepdims=True)) + lambda_init)
    row = lax.broadcasted_iota(jnp.int32, (LANES, tq), 0)
    part = jnp.zeros((LANES, tq), F32)
    for r in range(N_FEAT):
        part = jnp.where(row == r, LOG2E_PARTS[r % 3], part)
    feat_q = part * slope
    feat_ref[0] = feat_q.astype(BF16)
    feat_ref[1] = (-feat_q).astype(BF16)
    feat_ref[2] = jnp.zeros((LANES, tq), BF16)

    q0s = [(qi * tiles + t) * tq for t in range(tiles)]
    jds = [q0 // tk for q0 in q0s]

    for t in range(tiles):
        q_t = q_ref[t * tq:(t + 1) * tq, :].astype(F32).T * (hd ** -0.5 * LOG2E)
        qmap_ref[2 * t] = jnp.where(row < hd, q_t, 0.0).astype(BF16)
        qmap_ref[2 * t + 1] = jnp.where(row >= hd, q_t, 0.0).astype(BF16)
        ct = c_f32 * (q0s[t] + lax.broadcasted_iota(jnp.int32, (1, tq), 1)).astype(F32)
        r_ref[3 * t] = -ct
        r_ref[3 * t + 1] = ct
        r_ref[3 * t + 2] = jnp.zeros_like(ct)

    def latched(t, mp, side):
        return jnp.concatenate([qmap_ref[2 * t + mp], feat_ref[side]], axis=0)

    def block(t, idx):
        if idx == 0:
            return jds[t], 2
        j = idx - 1 + (idx - 1 >= jds[t]).astype(jnp.int32)
        return j, (j > jds[t]).astype(jnp.int32)

    def col_max(s_t):
        run = [s_t[8 * i:8 * i + 8, :] for i in range(4)]
        for i in range(4, tk // 8):
            run[i % 4] = jnp.maximum(run[i % 4], s_t[8 * i:8 * i + 8, :])
        top = jnp.maximum(jnp.maximum(run[0], run[1]), jnp.maximum(run[2], run[3]))
        return jnp.max(top, axis=0, keepdims=True)

    def diag_bias(t, j):
        s_pos = j * tk + lax.broadcasted_iota(jnp.int32, (tk, tq), 0)
        t_lane = q0s[t] + lax.broadcasted_iota(jnp.int32, (tk, tq), 1)
        return jnp.abs(s_pos - t_lane).astype(F32)

    def finalize(t):
        outs = []
        for mp in range(2):
            a = acc_ref[2 * t + mp]
            outs.append(a[0:LANES, :] * (1.0 / a[LANES:LANES + 1, :]))
        o_t = outs[0] - lam * outs[1]
        ms = jnp.mean(o_t * o_t, axis=0, keepdims=True)
        y = o_t * lax.rsqrt(ms + EPS) * (g_ref[...] * (1.0 - lambda_init))
        o_ref[t * tq:(t + 1) * tq, :] = y.T.astype(BF16)

    stream = [(t, idx, mp) for t in range(tiles) for idx in range(nk) for mp in range(2)]
    ref = [[None, None] for _ in range(tiles)]
    seen = [[[], []] for _ in range(tiles)]
    excess = [jnp.zeros((1, tq), F32) for _ in range(tiles)]

    def scores(unit):
        t, idx, mp = stream[unit]
        j, side = block(t, idx)
        s_t = _dot(ka_ref[j], latched(t, mp, side))
        return s_t - c_f32 * diag_bias(t, j) if idx == 0 else s_t

    pending = [scores(u) for u in range(min(ATTN_LOOKAHEAD, len(stream)))]
    for unit, (t, idx, mp) in enumerate(stream):
        j, side = block(t, idx)
        vt, r = vt_ref[j], r_ref[3 * t + side]
        s_t = pending.pop(0)
        if unit + ATTN_LOOKAHEAD < len(stream):
            pending.append(scores(unit + ATTN_LOOKAHEAD))
        bm = col_max(s_t) + r
        if idx == 0:
            new_ref = bm
            acc_ref[2 * t + mp] = _dot(vt, jnp.exp2(s_t - (new_ref - r)).astype(BF16))
        else:
            old_ref = ref[t][mp]
            new_ref = jnp.maximum(old_ref, seen[t][mp][max(idx - 2, 0)])
            p = jnp.exp2(s_t - (new_ref - r)).astype(BF16)
            acc_ref[2 * t + mp] = jnp.exp2(old_ref - new_ref) * acc_ref[2 * t + mp] + _dot(vt, p)
            excess[t] = jnp.maximum(excess[t], bm - new_ref)
        ref[t][mp] = new_ref
        seen[t][mp].append(bm)
        if idx == nk - 1 and mp == 1:
            finalize(t)

    for t in range(tiles):
        @pl.when(jnp.max(excess[t]) > ATTN_MAX_EXCESS)
        def _exact_tile(t=t):
            m_ref[...] = jnp.full(m_ref.shape, NEG_BIG, F32)
            for mp in range(2):
                acc_ref[2 * t + mp] = jnp.zeros((ATTN_VROWS, tq), F32)

            def body(j, carry):
                side = jnp.where(j < jds[t], 0, jnp.where(j == jds[t], 2, 1))
                ka, vt, r = ka_ref[j], vt_ref[j], r_ref[3 * t + side]
                bias = jnp.where(j == jds[t], c_f32, 0.0) * diag_bias(t, j)
                for mp in range(2):
                    s_t = _dot(ka, latched(t, mp, side)) - bias
                    m_old = m_ref[mp]
                    m_new = jnp.maximum(m_old, col_max(s_t) + r)
                    p = jnp.exp2(s_t - (m_new - r)).astype(BF16)
                    acc_ref[2 * t + mp] = jnp.exp2(m_old - m_new) * acc_ref[2 * t + mp] + _dot(vt, p)
                    m_ref[mp] = m_new
                return carry
            lax.fori_loop(0, nk, body, 0)
            finalize(t)


def _attn(proj, lq1, lk1, lq2, lk2, g_col, col0, lambda_init):
    B, S, _ = proj.shape
    tq, tk = min(ATTN_TQ, S), min(ATTN_TK, S)
    nk = S // tk
    tiles = max(1, min(ATTN_STREAM_BLOCKS // nk, S // tq))
    cb = col0 // LANES
    vec = pl.BlockSpec((1, LANES // 2), lambda b, h, i: (0, 0))
    return pl.pallas_call(
        functools.partial(_attn_kernel, seq=S, tq=tq, tk=tk, tiles=tiles, lambda_init=lambda_init),
        grid=(B, DIFF_HEADS, S // (tiles * tq)),
        in_specs=[
            pl.BlockSpec((None, tiles * tq, LANES), lambda b, h, i: (b, i, cb + h)),
            pl.BlockSpec((None, S, LANES), lambda b, h, i: (b, 0, cb + DIFF_HEADS + h)),
            pl.BlockSpec((None, S, LANES), lambda b, h, i: (b, 0, cb + 2 * DIFF_HEADS + h)),
            vec, vec, vec, vec,
            pl.BlockSpec((LANES, 1), lambda b, h, i: (0, 0)),
        ],
        out_specs=pl.BlockSpec((None, tiles * tq, LANES), lambda b, h, i: (b, i, h)),
        out_shape=jax.ShapeDtypeStruct((B, S, DIFF_HEADS * LANES), BF16),
        scratch_shapes=[
            pltpu.VMEM((nk, tk, 2 * LANES), BF16),
            pltpu.VMEM((nk, ATTN_VROWS, tk), BF16),
            pltpu.VMEM((2 * tiles, LANES, tq), BF16),
            pltpu.VMEM((3, LANES, tq), BF16),
            pltpu.VMEM((2 * tiles, ATTN_VROWS, tq), F32),
            pltpu.VMEM((2, 1, tq), F32),
            pltpu.VMEM((3 * tiles, 1, tq), F32),
        ],
        compiler_params=_params(3),
        name="attn",
    )(proj, proj, proj, lq1, lk1, lq2, lk2, g_col)


def _outproj_kernel(hf_ref, hb_ref, mo_ref, oa_ref, x_ref, gm_ref, w_ref, gp_ref, out_ref, *, hd):
    width = MLSTM_HEADS * hd
    ys = []
    for h in range(MLSTM_HEADS):
        sl = slice(h * hd, (h + 1) * hd)
        hm = _rms(hf_ref[:, sl].astype(F32) + hb_ref[:, sl].astype(F32), gm_ref[:, sl])
        ys.append((jax.nn.sigmoid(mo_ref[:, sl].astype(F32)) * hm).astype(BF16))
    ym = jnp.concatenate(ys, axis=1)
    mix = _dot(ym, w_ref[0:width, :]) + _dot(oa_ref[...], w_ref[width:, :])
    out_ref[...] = x_ref[...] + _rms(mix, gp_ref[...])


def _out_proj(hf, hb, proj, oa, x, gm, w_out, gp, width):
    B, S, D = x.shape
    tm = min(512, S)
    row = lambda b, i: (b, i, 0)
    const = lambda b, i: (0, 0)
    return pl.pallas_call(
        functools.partial(_outproj_kernel, hd=width // MLSTM_HEADS),
        grid=(B, S // tm),
        in_specs=[
            pl.BlockSpec((None, tm, width), row),
            pl.BlockSpec((None, tm, width), row),
            pl.BlockSpec((None, tm, width), lambda b, i: (b, i, 3)),
            pl.BlockSpec((None, tm, oa.shape[2]), row),
            pl.BlockSpec((None, tm, D), row),
            pl.BlockSpec((1, width), const),
            pl.BlockSpec(w_out.shape, const),
            pl.BlockSpec((1, D), const),
        ],
        out_specs=pl.BlockSpec((None, tm, D), row),
        out_shape=jax.ShapeDtypeStruct((B, S, D), F32),
        compiler_params=_params(2),
        name="out_proj",
    )(hf, hb, proj, oa, x, gm, w_out, gp)


def _mlp_kernel(x_ref, gpre_ref, wup_ref, wdown_ref, gpost_ref, out_ref, h_ref, acc_ref):
    f = pl.program_id(2)

    @pl.when(f == 0)
    def _():
        h_ref[...] = _rms(x_ref[...], gpre_ref[...]).astype(BF16)
        acc_ref[...] = jnp.zeros_like(acc_ref)

    u = jnp.maximum(_dot(h_ref[...], wup_ref[...]), 0.0)
    acc_ref[...] += _dot((u * u).astype(BF16), wdown_ref[...])

    @pl.when(f == pl.num_programs(2) - 1)
    def _():
        out_ref[...] = x_ref[...] + _rms(acc_ref[...], gpost_ref[...])


def _mlp(x, gpre, w_up, w_down, gpost):
    B, S, D = x.shape
    F = w_up.shape[1]
    tm = min(512, S)
    tf = 1024
    row = lambda b, i, f: (b, i, 0)
    const = lambda b, i, f: (0, 0)
    return pl.pallas_call(
        _mlp_kernel,
        grid=(B, S // tm, F // tf),
        in_specs=[
            pl.BlockSpec((None, tm, D), row),
            pl.BlockSpec((1, D), const),
            pl.BlockSpec((D, tf), lambda b, i, f: (0, f)),
            pl.BlockSpec((tf, D), lambda b, i, f: (f, 0)),
            pl.BlockSpec((1, D), const),
        ],
        out_specs=pl.BlockSpec((None, tm, D), row),
        out_shape=jax.ShapeDtypeStruct((B, S, D), F32),
        scratch_shapes=[pltpu.VMEM((tm, D), BF16), pltpu.VMEM((tm, D), F32)],
        compiler_params=_params(3),
        name="mlp",
    )(x, gpre, w_up, w_down, gpost)


def _layer(x, lambda_init, g_pre_mix, w_in, b_gates, mlstm_norm_g, lq1, lk1, lq2, lk2, subln_g,
           w_out, g_post_mix, g_pre_mlp, w_up, w_down, g_post_mlp):
    D = x.shape[-1]
    width = D // 2
    gate0 = 4 * width
    w_bf = w_in.astype(BF16)
    w_main = jnp.concatenate([w_bf[:, :gate0], w_bf[:, gate0 + N_GATES:]], axis=1)
    w_gt = w_bf[:, gate0:gate0 + N_GATES].T
    b_t = b_gates.reshape(N_GATES, 1)
    row = lambda v: v.reshape(1, -1)

    proj, lg, lgt = _in_proj(x, row(g_pre_mix), w_main, w_gt, b_t)
    hf, hb = _mlstm(proj, lg, lgt, width)
    oa = _attn(proj, row(lq1), row(lk1), row(lq2), row(lk2), subln_g.reshape(-1, 1), gate0, lambda_init)
    x1 = _out_proj(hf, hb, proj, oa, x, row(mlstm_norm_g), w_out.astype(BF16), row(g_post_mix), width)
    return _mlp(x1, row(g_pre_mlp), w_up.astype(BF16), w_down.astype(BF16), row(g_post_mlp))


def _trunk(x, *weights):
    depth = weights[0].shape[0]
    for l in range(depth):
        lambda_init = 0.8 - 0.6 * math.exp(-0.3 * l)
        x = _layer(x, lambda_init, *[w[l] for w in weights])
    return x


def kernel(x_prompt, x_sample, g_pre_mix, w_in, b_gates, mlstm_norm_g, lambda_q1, lambda_k1, lambda_q2,
           lambda_k2, subln_g, w_out, g_post_mix, g_pre_mlp, w_up, w_down, g_post_mlp):
    weights = (g_pre_mix, w_in, b_gates, mlstm_norm_g, lambda_q1, lambda_k1, lambda_q2, lambda_k2,
               subln_g, w_out, g_post_mix, g_pre_mlp, w_up, w_down, g_post_mlp)
    return (_trunk(x_prompt, *weights), _trunk(x_sample, *weights))
```

```python
import functools
import math
import struct

import jax
import jax.numpy as jnp
from jax import lax
from jax.experimental import pallas as pl
from jax.experimental.pallas import tpu as pltpu

F32 = jnp.float32
BF16 = jnp.bfloat16

EPS = 1e-6
MLSTM_HEADS = 4
DIFF_HEADS = 8
CHUNK = 256
MLSTM_ROWS = 1
MLSTM_LOOKAHEAD = 2
N_GATES = 4 * MLSTM_HEADS
LANES = 128
NEG_BIG = -1e30
VMEM_LIMIT = 56 * 1024 * 1024

ATTN_TQ = 256
ATTN_TK = 256
ATTN_STREAM_BLOCKS = 128
ATTN_VROWS = 144
N_FEAT = 6
ATTN_LOOKAHEAD = 3
ATTN_MAX_EXCESS = 64.0


def _f32_bits(x):
    return struct.unpack("<I", struct.pack("<f", x))[0]


def _round_f32(x):
    return struct.unpack("<f", struct.pack("<f", x))[0]


def _round_bf16(x):
    bits = _f32_bits(x)
    bits = (bits + 0x7FFF + ((bits >> 16) & 1)) & 0xFFFF0000
    return struct.unpack("<f", struct.pack("<I", bits))[0]


def _split3(value):
    parts, rest = [], value
    for _ in range(3):
        parts.append(_round_bf16(rest))
        rest = _round_f32(rest - parts[-1])
    assert rest == 0.0
    return parts


LOG2E = _round_f32(math.log2(math.e))
LOG2E_PARTS = _split3(LOG2E)


def _rms(x, g):
    return x * lax.rsqrt(jnp.mean(x * x, axis=-1, keepdims=True) + EPS) * g


def _log_sigmoid(x):
    return jnp.minimum(x, 0.0) - jnp.log1p(jnp.exp(-jnp.abs(x)))


def _dot(a, b):
    return jnp.dot(a, b, preferred_element_type=F32)


def _dot_nt(a, b):
    return lax.dot_general(a, b, (((1,), (1,)), ((), ())), preferred_element_type=F32)


def _params(n_axes, flags=None):
    return pltpu.CompilerParams(dimension_semantics=("arbitrary",) * n_axes,
                                vmem_limit_bytes=VMEM_LIMIT, flags=flags)


def _inproj_kernel(x_ref, g_ref, w_ref, wgt_ref, bt_ref, proj_ref, lg_ref, lgt_ref, hn_ref):
    @pl.when(pl.program_id(1) == 0)
    def _():
        hn = _rms(x_ref[...], g_ref[...]).astype(BF16)
        hn_ref[...] = hn
        gt = _dot_nt(wgt_ref[...], hn) + bt_ref[...]
        row = lax.broadcasted_iota(jnp.int32, gt.shape, 0)
        lgt = jnp.where((row & MLSTM_HEADS) != 0, _log_sigmoid(gt), gt)
        lgt_ref[...] = lgt
        pad = jnp.zeros((LANES - N_GATES, gt.shape[1]), F32)
        lg_ref[...] = jnp.concatenate([lgt, pad], axis=0).T

    proj_ref[...] = _dot(hn_ref[...], w_ref[...]).astype(BF16)


def _in_proj(x, g, w_main, w_gt, b_t):
    B, S, D = x.shape
    NM = w_main.shape[1]
    tm = min(1024, S)
    tn = NM // 4
    nst = S // tm
    return pl.pallas_call(
        _inproj_kernel,
        grid=(B * nst, NM // tn),
        in_specs=[
            pl.BlockSpec((None, tm, D), lambda i, j: (i // nst, i % nst, 0)),
            pl.BlockSpec((1, D), lambda i, j: (0, 0)),
            pl.BlockSpec((D, tn), lambda i, j: (0, j)),
            pl.BlockSpec((N_GATES, D), lambda i, j: (0, 0)),
            pl.BlockSpec((N_GATES, 1), lambda i, j: (0, 0)),
        ],
        out_specs=[
            pl.BlockSpec((None, tm, tn), lambda i, j: (i // nst, i % nst, j)),
            pl.BlockSpec((None, tm, LANES), lambda i, j: (i // nst, i % nst, 0)),
            pl.BlockSpec((None, N_GATES, tm), lambda i, j: (i // nst, 0, i % nst)),
        ],
        out_shape=[
            jax.ShapeDtypeStruct((B, S, NM), BF16),
            jax.ShapeDtypeStruct((B, S, LANES), F32),
            jax.ShapeDtypeStruct((B, N_GATES, S), F32),
        ],
        scratch_shapes=[pltpu.VMEM((tm, D), BF16)],
        compiler_params=_params(2),
        name="in_proj",
    )(x, g, w_main, w_gt, b_t)


def _split2(a):
    hi = a.astype(BF16)
    lo = (a - hi.astype(F32)).astype(BF16)
    return hi, lo


def _mlstm_gates(d, lg_ref, lgt_ref):
    L = CHUNK
    r_i = lax.broadcasted_iota(jnp.int32, (L, L), 0)
    c_i = lax.broadcasted_iota(jnp.int32, (L, L), 1)
    low, up = c_i <= r_i, c_i >= r_i
    mask = up if d else low
    m_col = mask.astype(BF16)
    m_row = (low if d else up).astype(BF16)
    lg = lg_ref[...]
    lgt = lgt_ref[...]
    lg_hi, lg_lo = _split2(lg)
    a_all = _dot(m_col, lg_hi) + _dot(m_col, lg_lo)
    lgt_hi, lgt_lo = _split2(lgt)
    b_all = _dot(lgt_hi, m_row) + _dot(lgt_lo, m_row)
    return mask, lgt, a_all, b_all


def _mlstm_scores(row, d, h, gates, p_ref, c_ref, m_ref, hd):
    mask, lgt, a_all, b_all = gates
    idx = (2 * row + d) * MLSTM_HEADS + h
    icol = 2 * d * MLSTM_HEADS + h
    fcol = icol + MLSTM_HEADS
    inv_sqrt = hd ** -0.5
    a_col = a_all[:, fcol:fcol + 1]
    r_row = lgt[icol:icol + 1, :] - b_all[fcol:fcol + 1, :]
    m_prev = m_ref[idx][0:1, 0:1]
    dmat = jnp.where(mask, a_col + r_row, NEG_BIG)
    inter = a_col + m_prev
    m_t = jnp.maximum(inter, jnp.max(dmat, axis=1, keepdims=True))
    w_intra = jnp.exp(dmat - m_t) * inv_sqrt
    w_inter = jnp.exp(inter - m_t) * inv_sqrt
    qb = p_ref[:, h * hd:(h + 1) * hd]
    kb = p_ref[:, (MLSTM_HEADS + h) * hd:(MLSTM_HEADS + h + 1) * hd]
    c_old = c_ref[idx]
    b_tot = jnp.sum(lgt[fcol:fcol + 1, :], axis=1, keepdims=True)
    g_row = b_tot + r_row
    m_new = jnp.maximum(b_tot + m_prev, jnp.max(g_row, axis=1, keepdims=True))
    decay = jnp.exp(b_tot + m_prev - m_new)
    ktw = (kb.astype(F32).T * jnp.exp(g_row - m_new)).astype(BF16)
    return dict(idx=idx, h=h, m_t=m_t, w_intra=w_intra, w_inter=w_inter, c_old=c_old,
                m_new=m_new, decay=decay, ktw=ktw,
                qk=_dot_nt(qb, kb), qc=_dot(qb, c_old.astype(BF16)))


def _mlstm_output(st, p_ref, out_ref, c_ref, m_ref, hd):
    L = CHUNK
    h, idx = st["h"], st["idx"]
    vb = p_ref[:, (2 * MLSTM_HEADS + h) * hd:(2 * MLSTM_HEADS + h + 1) * hd]
    v_aug = jnp.concatenate([vb, jnp.ones((L, LANES), BF16)], axis=1)
    s = st["qk"] * st["w_intra"]
    num = _dot(s.astype(BF16), v_aug) + st["w_inter"] * st["qc"]
    denom = jnp.maximum(jnp.abs(num[:, hd:hd + 1]), jnp.exp(-st["m_t"]))
    out_ref[:, h * hd:(h + 1) * hd] = (num[:, 0:hd] * (1.0 / denom)).astype(out_ref.dtype)

    c_ref[idx] = st["decay"] * st["c_old"] + _dot(st["ktw"], v_aug)
    m_ref[idx] = jnp.broadcast_to(st["m_new"], (8, LANES))


def _mlstm_kernel(pf_ref, pb_ref, lgf_ref, lgb_ref, lgtf_ref, lgtb_ref, hf_ref, hb_ref,
                  c_ref, m_ref, *, hd):
    @pl.when(pl.program_id(1) == 0)
    def _():
        c_ref[...] = jnp.zeros_like(c_ref)
        m_ref[...] = jnp.zeros_like(m_ref)

    rows = pf_ref.shape[0]
    p_refs, out_refs = (pf_ref, pb_ref), (hf_ref, hb_ref)
    lg_refs, lgt_refs = (lgf_ref, lgb_ref), (lgtf_ref, lgtb_ref)
    gates = [[_mlstm_gates(d, lg_refs[d].at[r], lgt_refs[d].at[r]) for d in range(2)] for r in range(rows)]
    items = [(r, d, h) for h in range(MLSTM_HEADS) for d in range(2) for r in range(rows)]
    first = lambda r, d, h: _mlstm_scores(r, d, h, gates[r][d], p_refs[d].at[r], c_ref, m_ref, hd)
    pending = [first(*it) for it in items[:MLSTM_LOOKAHEAD]]
    for i, (r, d, h) in enumerate(items):
        st = pending.pop(0)
        if i + MLSTM_LOOKAHEAD < len(items):
            pending.append(first(*items[i + MLSTM_LOOKAHEAD]))
        _mlstm_output(st, p_refs[d].at[r], out_refs[d].at[r], c_ref, m_ref, hd)


def _mlstm(proj, lg, lgt, width):
    B, S, _ = proj.shape
    nc = S // CHUNK
    hd = width // MLSTM_HEADS
    rows = MLSTM_ROWS if B % MLSTM_ROWS == 0 else 1
    fwd = lambda b, c: (b, c, 0)
    bwd = lambda b, c: (b, nc - 1 - c, 0)
    return pl.pallas_call(
        functools.partial(_mlstm_kernel, hd=hd),
        grid=(B // rows, nc),
        in_specs=[
            pl.BlockSpec((rows, CHUNK, 3 * width), fwd),
            pl.BlockSpec((rows, CHUNK, 3 * width), bwd),
            pl.BlockSpec((rows, CHUNK, LANES), fwd),
            pl.BlockSpec((rows, CHUNK, LANES), bwd),
            pl.BlockSpec((rows, N_GATES, CHUNK), lambda b, c: (b, 0, c)),
            pl.BlockSpec((rows, N_GATES, CHUNK), lambda b, c: (b, 0, nc - 1 - c)),
        ],
        out_specs=[
            pl.BlockSpec((rows, CHUNK, width), fwd),
            pl.BlockSpec((rows, CHUNK, width), bwd),
        ],
        out_shape=[jax.ShapeDtypeStruct((B, S, width), BF16)] * 2,
        scratch_shapes=[
            pltpu.VMEM((2 * rows * MLSTM_HEADS, hd, hd + LANES), F32),
            pltpu.VMEM((2 * rows * MLSTM_HEADS, 8, LANES), F32),
        ],
        compiler_params=_params(2),
        name="mlstm",
    )(proj, proj, lg, lg, lgt, lgt)


def _attn_kernel(q_ref, k_ref, v_ref, lq1_ref, lk1_ref, lq2_ref, lk2_ref, g_ref, o_ref,
                 ka_ref, vt_ref, qmap_ref, feat_ref, acc_ref, m_ref, r_ref,
                 *, seq, tq, tk, tiles, lambda_init):
    head = pl.program_id(1)
    qi = pl.program_id(2)
    nk = seq // tk
    hd = LANES // 2
    slope = pltpu.bitcast(jnp.full((8, LANES), (126 - head) << 23, jnp.int32), F32)[0:1, 0:1]
    c_f32 = slope * LOG2E

    @pl.when((pl.program_id(0) == 0) & (head == 0) & (qi == 0))
    def _build_constants():
        def body(j, carry):
            pos = j * tk + lax.broadcasted_iota(jnp.int32, (tk, LANES), 0)
            lane = lax.broadcasted_iota(jnp.int32, (tk, LANES), 1)
            s_lo = pos & (LANES - 1)
            feat = jnp.where(lane < 3, s_lo, jnp.where(lane < N_FEAT, pos - s_lo, 0))
            ka_ref[j, :, LANES:2 * LANES] = feat.astype(F32).astype(BF16)
            ones_row = lax.broadcasted_iota(jnp.int32, (ATTN_VROWS - LANES, tk), 0) == 0
            vt_ref[j, LANES:ATTN_VROWS, :] = jnp.where(ones_row, 1.0, 0.0).astype(BF16)
            return carry
        lax.fori_loop(0, nk, body, 0)

    @pl.when(qi == 0)
    def _build_keys():
        def body(j, carry):
            start = pl.multiple_of(j * tk, tk)
            ka_ref[j, :, 0:LANES] = k_ref[pl.ds(start, tk), :]
            vt_ref[j, 0:LANES, :] = v_ref[pl.ds(start, tk), :].astype(F32).T.astype(BF16)
            return carry
        lax.fori_loop(0, nk, body, 0)

    lam = (jnp.exp(jnp.sum(lq1_ref[...] * lk1_ref[...], axis=1, keepdims=True))
           - jnp.exp(jnp.sum(lq2_ref[...] * lk2_ref[...], axis=1, keepdims=True)) + lambda_init)
    row = lax.broadcasted_iota(jnp.int32, (LANES, tq), 0)
    part = jnp.zeros((LANES, tq), F32)
    for r in range(N_FEAT):
        part = jnp.where(row == r, LOG2E_PARTS[r % 3], part)
    feat_q = part * slope
    feat_ref[0] = feat_q.astype(BF16)
    feat_ref[1] = (-feat_q).astype(BF16)
    feat_ref[2] = jnp.zeros((LANES, tq), BF16)

    q0s = [(qi * tiles + t) * tq for t in range(tiles)]
    jds = [q0 // tk for q0 in q0s]

    for t in range(tiles):
        q_t = q_ref[t * tq:(t + 1) * tq, :].astype(F32).T * (hd ** -0.5 * LOG2E)
        qmap_ref[2 * t] = jnp.where(row < hd, q_t, 0.0).astype(BF16)
        qmap_ref[2 * t + 1] = jnp.where(row >= hd, q_t, 0.0).astype(BF16)
        ct = c_f32 * (q0s[t] + lax.broadcasted_iota(jnp.int32, (1, tq), 1)).astype(F32)
        r_ref[3 * t] = -ct
        r_ref[3 * t + 1] = ct
        r_ref[3 * t + 2] = jnp.zeros_like(ct)

    def latched(t, mp, side):
        return jnp.concatenate([qmap_ref[2 * t + mp], feat_ref[side]], axis=0)

    def block(t, idx):
        if idx == 0:
            return jds[t], 2
        j = idx - 1 + (idx - 1 >= jds[t]).astype(jnp.int32)
        return j, (j > jds[t]).astype(jnp.int32)

    def col_max(s_t):
        run = [s_t[8 * i:8 * i + 8, :] for i in range(4)]
        for i in range(4, tk // 8):
            run[i % 4] = jnp.maximum(run[i % 4], s_t[8 * i:8 * i + 8, :])
        top = jnp.maximum(jnp.maximum(run[0], run[1]), jnp.maximum(run[2], run[3]))
        return jnp.max(top, axis=0, keepdims=True)

    def diag_bias(t, j):
        s_pos = j * tk + lax.broadcasted_iota(jnp.int32, (tk, tq), 0)
        t_lane = q0s[t] + lax.broadcasted_iota(jnp.int32, (tk, tq), 1)
        return jnp.abs(s_pos - t_lane).astype(F32)

    def finalize(t):
        outs = []
        for mp in range(2):
            a = acc_ref[2 * t + mp]
            outs.append(a[0:LANES, :] * (1.0 / a[LANES:LANES + 1, :]))
        o_t = outs[0] - lam * outs[1]
        ms = jnp.mean(o_t * o_t, axis=0, keepdims=True)
        y = o_t * lax.rsqrt(ms + EPS) * (g_ref[...] * (1.0 - lambda_init))
        o_ref[t * tq:(t + 1) * tq, :] = y.T.astype(BF16)

    stream = [(t, idx, mp) for t in range(tiles) for idx in range(nk) for mp in range(2)]
    ref = [[None, None] for _ in range(tiles)]
    seen = [[[], []] for _ in range(tiles)]
    excess = [jnp.zeros((1, tq), F32) for _ in range(tiles)]

    def scores(unit):
        t, idx, mp = stream[unit]
        j, side = block(t, idx)
        s_t = _dot(ka_ref[j], latched(t, mp, side))
        return s_t - c_f32 * diag_bias(t, j) if idx == 0 else s_t

    pending = [scores(u) for u in range(min(ATTN_LOOKAHEAD, len(stream)))]
    for unit, (t, idx, mp) in enumerate(stream):
        j, side = block(t, idx)
        vt, r = vt_ref[j], r_ref[3 * t + side]
        s_t = pending.pop(0)
        if unit + ATTN_LOOKAHEAD < len(stream):
            pending.append(scores(unit + ATTN_LOOKAHEAD))
        bm = col_max(s_t) + r
        if idx == 0:
            new_ref = bm
            acc_ref[2 * t + mp] = _dot(vt, jnp.exp2(s_t - (new_ref - r)).astype(BF16))
        else:
            old_ref = ref[t][mp]
            new_ref = jnp.maximum(old_ref, seen[t][mp][max(idx - 2, 0)])
            p = jnp.exp2(s_t - (new_ref - r)).astype(BF16)
            acc_ref[2 * t + mp] = jnp.exp2(old_ref - new_ref) * acc_ref[2 * t + mp] + _dot(vt, p)
            excess[t] = jnp.maximum(excess[t], bm - new_ref)
        ref[t][mp] = new_ref
        seen[t][mp].append(bm)
        if idx == nk - 1 and mp == 1:
            finalize(t)

    for t in range(tiles):
        @pl.when(jnp.max(excess[t]) > ATTN_MAX_EXCESS)
        def _exact_tile(t=t):
            m_ref[...] = jnp.full(m_ref.shape, NEG_BIG, F32)
            for mp in range(2):
                acc_ref[2 * t + mp] = jnp.zeros((ATTN_VROWS, tq), F32)

            def body(j, carry):
                side = jnp.where(j < jds[t], 0, jnp.where(j == jds[t], 2, 1))
                ka, vt, r = ka_ref[j], vt_ref[j], r_ref[3 * t + side]
                bias = jnp.where(j == jds[t], c_f32, 0.0) * diag_bias(t, j)
                for mp in range(2):
                    s_t = _dot(ka, latched(t, mp, side)) - bias
                    m_old = m_ref[mp]
                    m_new = jnp.maximum(m_old, col_max(s_t) + r)
                    p = jnp.exp2(s_t - (m_new - r)).astype(BF16)
                    acc_ref[2 * t + mp] = jnp.exp2(m_old - m_new) * acc_ref[2 * t + mp] + _dot(vt, p)
                    m_ref[mp] = m_new
                return carry
            lax.fori_loop(0, nk, body, 0)
            finalize(t)


def _attn(proj, lq1, lk1, lq2, lk2, g_col, col0, lambda_init):
    B, S, _ = proj.shape
    tq, tk = min(ATTN_TQ, S), min(ATTN_TK, S)
    nk = S // tk
    tiles = max(1, min(ATTN_STREAM_BLOCKS // nk, S // tq))
    cb = col0 // LANES
    vec = pl.BlockSpec((1, LANES // 2), lambda b, h, i: (0, 0))
    return pl.pallas_call(
        functools.partial(_attn_kernel, seq=S, tq=tq, tk=tk, tiles=tiles, lambda_init=lambda_init),
        grid=(B, DIFF_HEADS, S // (tiles * tq)),
        in_specs=[
            pl.BlockSpec((None, tiles * tq, LANES), lambda b, h, i: (b, i, cb + h)),
            pl.BlockSpec((None, S, LANES), lambda b, h, i: (b, 0, cb + DIFF_HEADS + h)),
            pl.BlockSpec((None, S, LANES), lambda b, h, i: (b, 0, cb + 2 * DIFF_HEADS + h)),
            vec, vec, vec, vec,
            pl.BlockSpec((LANES, 1), lambda b, h, i: (0, 0)),
        ],
        out_specs=pl.BlockSpec((None, tiles * tq, LANES), lambda b, h, i: (b, i, h)),
        out_shape=jax.ShapeDtypeStruct((B, S, DIFF_HEADS * LANES), BF16),
        scratch_shapes=[
            pltpu.VMEM((nk, tk, 2 * LANES), BF16),
            pltpu.VMEM((nk, ATTN_VROWS, tk), BF16),
            pltpu.VMEM((2 * tiles, LANES, tq), BF16),
            pltpu.VMEM((3, LANES, tq), BF16),
            pltpu.VMEM((2 * tiles, ATTN_VROWS, tq), F32),
            pltpu.VMEM((2, 1, tq), F32),
            pltpu.VMEM((3 * tiles, 1, tq), F32),
        ],
        compiler_params=_params(3),
        name="attn",
    )(proj, proj, proj, lq1, lk1, lq2, lk2, g_col)


def _outproj_kernel(hf_ref, hb_ref, mo_ref, oa_ref, x_ref, gm_ref, w_ref, gp_ref, out_ref, *, hd):
    width = MLSTM_HEADS * hd
    ys = []
    for h in range(MLSTM_HEADS):
        sl = slice(h * hd, (h + 1) * hd)
        hm = _rms(hf_ref[:, sl].astype(F32) + hb_ref[:, sl].astype(F32), gm_ref[:, sl])
        ys.append((jax.nn.sigmoid(mo_ref[:, sl].astype(F32)) * hm).astype(BF16))
    ym = jnp.concatenate(ys, axis=1)
    mix = _dot(ym, w_ref[0:width, :]) + _dot(oa_ref[...], w_ref[width:, :])
    out_ref[...] = x_ref[...] + _rms(mix, gp_ref[...])


def _out_proj(hf, hb, proj, oa, x, gm, w_out, gp, width):
    B, S, D = x.shape
    tm = min(512, S)
    row = lambda b, i: (b, i, 0)
    const = lambda b, i: (0, 0)
    return pl.pallas_call(
        functools.partial(_outproj_kernel, hd=width // MLSTM_HEADS),
        grid=(B, S // tm),
        in_specs=[
            pl.BlockSpec((None, tm, width), row),
            pl.BlockSpec((None, tm, width), row),
            pl.BlockSpec((None, tm, width), lambda b, i: (b, i, 3)),
            pl.BlockSpec((None, tm, oa.shape[2]), row),
            pl.BlockSpec((None, tm, D), row),
            pl.BlockSpec((1, width), const),
            pl.BlockSpec(w_out.shape, const),
            pl.BlockSpec((1, D), const),
        ],
        out_specs=pl.BlockSpec((None, tm, D), row),
        out_shape=jax.ShapeDtypeStruct((B, S, D), F32),
        compiler_params=_params(2),
        name="out_proj",
    )(hf, hb, proj, oa, x, gm, w_out, gp)


def _mlp_kernel(x_ref, gpre_ref, wup_ref, wdown_ref, gpost_ref, out_ref, h_ref, acc_ref):
    f = pl.program_id(2)

    @pl.when(f == 0)
    def _():
        h_ref[...] = _rms(x_ref[...], gpre_ref[...]).astype(BF16)
        acc_ref[...] = jnp.zeros_like(acc_ref)

    u = jnp.maximum(_dot(h_ref[...], wup_ref[...]), 0.0)
    acc_ref[...] += _dot((u * u).astype(BF16), wdown_ref[...])

    @pl.when(f == pl.num_programs(2) - 1)
    def _():
        out_ref[...] = x_ref[...] + _rms(acc_ref[...], gpost_ref[...])


def _mlp(x, gpre, w_up, w_down, gpost):
    B, S, D = x.shape
    F = w_up.shape[1]
    tm = min(512, S)
    tf = 1024
    row = lambda b, i, f: (b, i, 0)
    const = lambda b, i, f: (0, 0)
    return pl.pallas_call(
        _mlp_kernel,
        grid=(B, S // tm, F // tf),
        in_specs=[
            pl.BlockSpec((None, tm, D), row),
            pl.BlockSpec((1, D), const),
            pl.BlockSpec((D, tf), lambda b, i, f: (0, f)),
            pl.BlockSpec((tf, D), lambda b, i, f: (f, 0)),
            pl.BlockSpec((1, D), const),
        ],
        out_specs=pl.BlockSpec((None, tm, D), row),
        out_shape=jax.ShapeDtypeStruct((B, S, D), F32),
        scratch_shapes=[pltpu.VMEM((tm, D), BF16), pltpu.VMEM((tm, D), F32)],
        compiler_params=_params(3),
        name="mlp",
    )(x, gpre, w_up, w_down, gpost)


def _layer(x, lambda_init, g_pre_mix, w_in, b_gates, mlstm_norm_g, lq1, lk1, lq2, lk2, subln_g,
           w_out, g_post_mix, g_pre_mlp, w_up, w_down, g_post_mlp):
    D = x.shape[-1]
    width = D // 2
    gate0 = 4 * width
    w_bf = w_in.astype(BF16)
    w_main = jnp.concatenate([w_bf[:, :gate0], w_bf[:, gate0 + N_GATES:]], axis=1)
    w_gt = w_bf[:, gate0:gate0 + N_GATES].T
    b_t = b_gates.reshape(N_GATES, 1)
    row = lambda v: v.reshape(1, -1)

    proj, lg, lgt = _in_proj(x, row(g_pre_mix), w_main, w_gt, b_t)
    hf, hb = _mlstm(proj, lg, lgt, width)
    oa = _attn(proj, row(lq1), row(lk1), row(lq2), row(lk2), subln_g.reshape(-1, 1), gate0, lambda_init)
    x1 = _out_proj(hf, hb, proj, oa, x, row(mlstm_norm_g), w_out.astype(BF16), row(g_post_mix), width)
    return _mlp(x1, row(g_pre_mlp), w_up.astype(BF16), w_down.astype(BF16), row(g_post_mlp))


def _trunk(x, *weights):
    depth = weights[0].shape[0]
    for l in range(depth):
        lambda_init = 0.8 - 0.6 * math.exp(-0.3 * l)
        x = _layer(x, lambda_init, *[w[l] for w in weights])
    return x


def kernel(x_prompt, x_sample, g_pre_mix, w_in, b_gates, mlstm_norm_g, lambda_q1, lambda_k1, lambda_q2,
           lambda_k2, subln_g, w_out, g_post_mix, g_pre_mlp, w_up, w_down, g_post_mlp):
    weights = (g_pre_mix, w_in, b_gates, mlstm_norm_g, lambda_q1, lambda_k1, lambda_q2, lambda_k2,
               subln_g, w_out, g_post_mix, g_pre_mlp, w_up, w_down, g_post_mlp)
    return (_trunk(x_prompt, *weights), _trunk(x_sample, *weights))
```

```python
import functools
import math
import struct

import jax
import jax.numpy as jnp
from jax import lax
from jax.experimental import pallas as pl
from jax.experimental.pallas import tpu as pltpu

F32 = jnp.float32
BF16 = jnp.bfloat16

EPS = 1e-6
MLSTM_HEADS = 4
DIFF_HEADS = 8
CHUNK = 256
MLSTM_ROWS = 1
MLSTM_LOOKAHEAD = 2
N_GATES = 4 * MLSTM_HEADS
LANES = 128
NEG_BIG = -1e30
VMEM_LIMIT = 56 * 1024 * 1024
MLP_CHUNK = 512

ATTN_TQ = 256
ATTN_TK = 256
ATTN_STREAM_BLOCKS = 128
ATTN_VROWS = 144
N_FEAT = 6
ATTN_LOOKAHEAD = 4
ATTN_MAX_EXCESS = 64.0


def _f32_bits(x):
    return struct.unpack("<I", struct.pack("<f", x))[0]


def _round_f32(x):
    return struct.unpack("<f", struct.pack("<f", x))[0]


def _round_bf16(x):
    bits = _f32_bits(x)
    bits = (bits + 0x7FFF + ((bits >> 16) & 1)) & 0xFFFF0000
    return struct.unpack("<f", struct.pack("<I", bits))[0]


def _split3(value):
    parts, rest = [], value
    for _ in range(3):
        parts.append(_round_bf16(rest))
        rest = _round_f32(rest - parts[-1])
    assert rest == 0.0
    return parts


LOG2E = _round_f32(math.log2(math.e))
LOG2E_PARTS = _split3(LOG2E)


def _rms(x, g):
    return x * lax.rsqrt(jnp.mean(x * x, axis=-1, keepdims=True) + EPS) * g


def _log_sigmoid(x):
    return jnp.minimum(x, 0.0) - jnp.log1p(jnp.exp(-jnp.abs(x)))


def _dot(a, b):
    return jnp.dot(a, b, preferred_element_type=F32)


def _dot_nt(a, b):
    return lax.dot_general(a, b, (((1,), (1,)), ((), ())), preferred_element_type=F32)


def _params(n_axes, flags=None):
    return pltpu.CompilerParams(dimension_semantics=("arbitrary",) * n_axes,
                                vmem_limit_bytes=VMEM_LIMIT, flags=flags)


def _inproj_kernel(x_ref, g_ref, w_ref, wgt_ref, bt_ref, proj_ref, lg_ref, lgt_ref, hn_ref):
    @pl.when(pl.program_id(1) == 0)
    def _():
        hn = _rms(x_ref[...], g_ref[...]).astype(BF16)
        hn_ref[...] = hn
        gt = _dot_nt(wgt_ref[...], hn) + bt_ref[...]
        row = lax.broadcasted_iota(jnp.int32, gt.shape, 0)
        lgt = jnp.where((row & MLSTM_HEADS) != 0, _log_sigmoid(gt), gt)
        lgt_ref[...] = lgt
        pad = jnp.zeros((LANES - N_GATES, gt.shape[1]), F32)
        lg_ref[...] = jnp.concatenate([lgt, pad], axis=0).T

    proj_ref[...] = _dot(hn_ref[...], w_ref[...]).astype(BF16)


def _in_proj(x, g, w_main, w_gt, b_t):
    B, S, D = x.shape
    NM = w_main.shape[1]
    tm = min(1024, S)
    tn = NM // 4
    nst = S // tm
    return pl.pallas_call(
        _inproj_kernel,
        grid=(B * nst, NM // tn),
        in_specs=[
            pl.BlockSpec((None, tm, D), lambda i, j: (i // nst, i % nst, 0)),
            pl.BlockSpec((1, D), lambda i, j: (0, 0)),
            pl.BlockSpec((D, tn), lambda i, j: (0, j)),
            pl.BlockSpec((N_GATES, D), lambda i, j: (0, 0)),
            pl.BlockSpec((N_GATES, 1), lambda i, j: (0, 0)),
        ],
        out_specs=[
            pl.BlockSpec((None, tm, tn), lambda i, j: (i // nst, i % nst, j)),
            pl.BlockSpec((None, tm, LANES), lambda i, j: (i // nst, i % nst, 0)),
            pl.BlockSpec((None, N_GATES, tm), lambda i, j: (i // nst, 0, i % nst)),
        ],
        out_shape=[
            jax.ShapeDtypeStruct((B, S, NM), BF16),
            jax.ShapeDtypeStruct((B, S, LANES), F32),
            jax.ShapeDtypeStruct((B, N_GATES, S), F32),
        ],
        scratch_shapes=[pltpu.VMEM((tm, D), BF16)],
        compiler_params=_params(2),
        name="in_proj",
    )(x, g, w_main, w_gt, b_t)


def _split2(a):
    hi = a.astype(BF16)
    lo = (a - hi.astype(F32)).astype(BF16)
    return hi, lo


def _mlstm_gates(d, lg_ref, lgt_ref):
    L = CHUNK
    r_i = lax.broadcasted_iota(jnp.int32, (L, L), 0)
    c_i = lax.broadcasted_iota(jnp.int32, (L, L), 1)
    low, up = c_i <= r_i, c_i >= r_i
    mask = up if d else low
    m_col = mask.astype(BF16)
    m_row = (low if d else up).astype(BF16)
    lg = lg_ref[...]
    lgt = lgt_ref[...]
    lg_hi, lg_lo = _split2(lg)
    a_all = _dot(m_col, lg_hi) + _dot(m_col, lg_lo)
    lgt_hi, lgt_lo = _split2(lgt)
    b_all = _dot(lgt_hi, m_row) + _dot(lgt_lo, m_row)
    return mask, lgt, a_all, b_all


def _mlstm_scores(row, d, h, gates, p_ref, c_ref, m_ref, hd):
    mask, lgt, a_all, b_all = gates
    idx = (2 * row + d) * MLSTM_HEADS + h
    icol = 2 * d * MLSTM_HEADS + h
    fcol = icol + MLSTM_HEADS
    inv_sqrt = hd ** -0.5
    a_col = a_all[:, fcol:fcol + 1]
    r_row = lgt[icol:icol + 1, :] - b_all[fcol:fcol + 1, :]
    m_prev = m_ref[idx][0:1, 0:1]
    dmat = jnp.where(mask, a_col + r_row, NEG_BIG)
    inter = a_col + m_prev
    m_t = jnp.maximum(inter, jnp.max(dmat, axis=1, keepdims=True))
    w_intra = jnp.exp(dmat - m_t) * inv_sqrt
    w_inter = jnp.exp(inter - m_t) * inv_sqrt
    qb = p_ref[:, h * hd:(h + 1) * hd]
    kb = p_ref[:, (MLSTM_HEADS + h) * hd:(MLSTM_HEADS + h + 1) * hd]
    c_old = c_ref[idx]
    b_tot = jnp.sum(lgt[fcol:fcol + 1, :], axis=1, keepdims=True)
    g_row = b_tot + r_row
    m_new = jnp.maximum(b_tot + m_prev, jnp.max(g_row, axis=1, keepdims=True))
    decay = jnp.exp(b_tot + m_prev - m_new)
    ktw = (kb.astype(F32).T * jnp.exp(g_row - m_new)).astype(BF16)
    return dict(idx=idx, h=h, m_t=m_t, w_intra=w_intra, w_inter=w_inter, c_old=c_old,
                m_new=m_new, decay=decay, ktw=ktw,
                qk=_dot_nt(qb, kb), qc=_dot(qb, c_old.astype(BF16)))


def _mlstm_output(st, p_ref, out_ref, c_ref, m_ref, hd):
    L = CHUNK
    h, idx = st["h"], st["idx"]
    vb = p_ref[:, (2 * MLSTM_HEADS + h) * hd:(2 * MLSTM_HEADS + h + 1) * hd]
    v_aug = jnp.concatenate([vb, jnp.ones((L, LANES), BF16)], axis=1)
    s = st["qk"] * st["w_intra"]
    num = _dot(s.astype(BF16), v_aug) + st["w_inter"] * st["qc"]
    denom = jnp.maximum(jnp.abs(num[:, hd:hd + 1]), jnp.exp(-st["m_t"]))
    out_ref[:, h * hd:(h + 1) * hd] = (num[:, 0:hd] * (1.0 / denom)).astype(out_ref.dtype)

    c_ref[idx] = st["decay"] * st["c_old"] + _dot(st["ktw"], v_aug)
    m_ref[idx] = jnp.broadcast_to(st["m_new"], (8, LANES))


def _mlstm_kernel(pf_ref, pb_ref, lgf_ref, lgb_ref, lgtf_ref, lgtb_ref, hf_ref, hb_ref,
                  c_ref, m_ref, *, hd):
    @pl.when(pl.program_id(1) == 0)
    def _():
        c_ref[...] = jnp.zeros_like(c_ref)
        m_ref[...] = jnp.zeros_like(m_ref)

    rows = pf_ref.shape[0]
    p_refs, out_refs = (pf_ref, pb_ref), (hf_ref, hb_ref)
    lg_refs, lgt_refs = (lgf_ref, lgb_ref), (lgtf_ref, lgtb_ref)
    gates = [[_mlstm_gates(d, lg_refs[d].at[r], lgt_refs[d].at[r]) for d in range(2)] for r in range(rows)]
    items = [(r, d, h) for h in range(MLSTM_HEADS) for d in range(2) for r in range(rows)]
    first = lambda r, d, h: _mlstm_scores(r, d, h, gates[r][d], p_refs[d].at[r], c_ref, m_ref, hd)
    pending = [first(*it) for it in items[:MLSTM_LOOKAHEAD]]
    for i, (r, d, h) in enumerate(items):
        st = pending.pop(0)
        if i + MLSTM_LOOKAHEAD < len(items):
            pending.append(first(*items[i + MLSTM_LOOKAHEAD]))
        _mlstm_output(st, p_refs[d].at[r], out_refs[d].at[r], c_ref, m_ref, hd)


def _mlstm(proj, lg, lgt, width):
    B, S, _ = proj.shape
    nc = S // CHUNK
    hd = width // MLSTM_HEADS
    rows = MLSTM_ROWS if B % MLSTM_ROWS == 0 else 1
    fwd = lambda b, c: (b, c, 0)
    bwd = lambda b, c: (b, nc - 1 - c, 0)
    return pl.pallas_call(
        functools.partial(_mlstm_kernel, hd=hd),
        grid=(B // rows, nc),
        in_specs=[
            pl.BlockSpec((rows, CHUNK, 3 * width), fwd),
            pl.BlockSpec((rows, CHUNK, 3 * width), bwd),
            pl.BlockSpec((rows, CHUNK, LANES), fwd),
            pl.BlockSpec((rows, CHUNK, LANES), bwd),
            pl.BlockSpec((rows, N_GATES, CHUNK), lambda b, c: (b, 0, c)),
            pl.BlockSpec((rows, N_GATES, CHUNK), lambda b, c: (b, 0, nc - 1 - c)),
        ],
        out_specs=[
            pl.BlockSpec((rows, CHUNK, width), fwd),
            pl.BlockSpec((rows, CHUNK, width), bwd),
        ],
        out_shape=[jax.ShapeDtypeStruct((B, S, width), BF16)] * 2,
        scratch_shapes=[
            pltpu.VMEM((2 * rows * MLSTM_HEADS, hd, hd + LANES), F32),
            pltpu.VMEM((2 * rows * MLSTM_HEADS, 8, LANES), F32),
        ],
        compiler_params=_params(2),
        name="mlstm",
    )(proj, proj, lg, lg, lgt, lgt)


def _attn_kernel(q_ref, k_ref, v_ref, lq1_ref, lk1_ref, lq2_ref, lk2_ref, g_ref, o_ref,
                 ka_ref, vt_ref, qmap_ref, feat_ref, acc_ref, m_ref, r_ref,
                 *, seq, tq, tk, tiles, lambda_init):
    head = pl.program_id(1)
    qi = pl.program_id(2)
    nk = seq // tk
    hd = LANES // 2
    slope = pltpu.bitcast(jnp.full((8, LANES), (126 - head) << 23, jnp.int32), F32)[0:1, 0:1]
    c_f32 = slope * LOG2E

    @pl.when((pl.program_id(0) == 0) & (head == 0) & (qi == 0))
    def _build_constants():
        def body(j, carry):
            pos = j * tk + lax.broadcasted_iota(jnp.int32, (tk, LANES), 0)
            lane = lax.broadcasted_iota(jnp.int32, (tk, LANES), 1)
            s_lo = pos & (LANES - 1)
            feat = jnp.where(lane < 3, s_lo, jnp.where(lane < N_FEAT, pos - s_lo, 0))
            ka_ref[j, :, LANES:2 * LANES] = feat.astype(F32).astype(BF16)
            ones_row = lax.broadcasted_iota(jnp.int32, (ATTN_VROWS - LANES, tk), 0) == 0
            vt_ref[j, LANES:ATTN_VROWS, :] = jnp.where(ones_row, 1.0, 0.0).astype(BF16)
            return carry
        lax.fori_loop(0, nk, body, 0)

    @pl.when(qi == 0)
    def _build_keys():
        def body(j, carry):
            start = pl.multiple_of(j * tk, tk)
            ka_ref[j, :, 0:LANES] = k_ref[pl.ds(start, tk), :]
            vt_ref[j, 0:LANES, :] = v_ref[pl.ds(start, tk), :].astype(F32).T.astype(BF16)
            return carry
        lax.fori_loop(0, nk, body, 0)

    lam = (jnp.exp(jnp.sum(lq1_ref[...] * lk1_ref[...], axis=1, keepdims=True))
           - jnp.exp(jnp.sum(lq2_ref[...] * lk2_ref[...], axis=1, keepdims=True)) + lambda_init)
    row = lax.broadcasted_iota(jnp.int32, (LANES, tq), 0)
    part = jnp.zeros((LANES, tq), F32)
    for r in range(N_FEAT):
        part = jnp.where(row == r, LOG2E_PARTS[r % 3], part)
    feat_q = part * slope
    feat_ref[0] = feat_q.astype(BF16)
    feat_ref[1] = (-feat_q).astype(BF16)
    feat_ref[2] = jnp.zeros((LANES, tq), BF16)

    q0s = [(qi * tiles + t) * tq for t in range(tiles)]
    jds = [q0 // tk for q0 in q0s]

    for t in range(tiles):
        q_t = q_ref[t * tq:(t + 1) * tq, :].astype(F32).T * (hd ** -0.5 * LOG2E)
        qmap_ref[2 * t] = jnp.where(row < hd, q_t, 0.0).astype(BF16)
        qmap_ref[2 * t + 1] = jnp.where(row >= hd, q_t, 0.0).astype(BF16)
        ct = c_f32 * (q0s[t] + lax.broadcasted_iota(jnp.int32, (1, tq), 1)).astype(F32)
        r_ref[3 * t] = -ct
        r_ref[3 * t + 1] = ct
        r_ref[3 * t + 2] = jnp.zeros_like(ct)

    def latched(t, mp, side):
        return jnp.concatenate([qmap_ref[2 * t + mp], feat_ref[side]], axis=0)

    def block(t, idx):
        if idx == 0:
            return jds[t], 2
        j = idx - 1 + (idx - 1 >= jds[t]).astype(jnp.int32)
        return j, (j > jds[t]).astype(jnp.int32)

    def col_max(s_t):
        run = [s_t[8 * i:8 * i + 8, :] for i in range(4)]
        for i in range(4, tk // 8):
            run[i % 4] = jnp.maximum(run[i % 4], s_t[8 * i:8 * i + 8, :])
        top = jnp.maximum(jnp.maximum(run[0], run[1]), jnp.maximum(run[2], run[3]))
        return jnp.max(top, axis=0, keepdims=True)

    def diag_bias(t, j):
        s_pos = j * tk + lax.broadcasted_iota(jnp.int32, (tk, tq), 0)
        t_lane = q0s[t] + lax.broadcasted_iota(jnp.int32, (tk, tq), 1)
        return jnp.abs(s_pos - t_lane).astype(F32)

    def finalize(t):
        outs = []
        for mp in range(2):
            a = acc_ref[2 * t + mp]
            outs.append(a[0:LANES, :] * (1.0 / a[LANES:LANES + 1, :]))
        o_t = outs[0] - lam * outs[1]
        ms = jnp.mean(o_t * o_t, axis=0, keepdims=True)
        y = o_t * lax.rsqrt(ms + EPS) * (g_ref[...] * (1.0 - lambda_init))
        o_ref[t * tq:(t + 1) * tq, :] = y.T.astype(BF16)

    stream = [(t, idx, mp) for t in range(tiles) for idx in range(nk) for mp in range(2)]
    ref = [[None, None] for _ in range(tiles)]
    seen = [[[], []] for _ in range(tiles)]
    excess = [jnp.zeros((1, tq), F32) for _ in range(tiles)]

    def scores(unit):
        t, idx, mp = stream[unit]
        j, side = block(t, idx)
        s_t = _dot(ka_ref[j], latched(t, mp, side))
        return s_t - c_f32 * diag_bias(t, j) if idx == 0 else s_t

    pending = [scores(u) for u in range(min(ATTN_LOOKAHEAD, len(stream)))]
    for unit, (t, idx, mp) in enumerate(stream):
        j, side = block(t, idx)
        vt, r = vt_ref[j], r_ref[3 * t + side]
        s_t = pending.pop(0)
        if unit + ATTN_LOOKAHEAD < len(stream):
            pending.append(scores(unit + ATTN_LOOKAHEAD))
        bm = col_max(s_t) + r
        if idx == 0:
            new_ref = bm
            acc_ref[2 * t + mp] = _dot(vt, jnp.exp2(s_t - (new_ref - r)).astype(BF16))
        else:
            old_ref = ref[t][mp]
            new_ref = jnp.maximum(old_ref, seen[t][mp][max(idx - 2, 0)])
            p = jnp.exp2(s_t - (new_ref - r)).astype(BF16)
            acc_ref[2 * t + mp] = jnp.exp2(old_ref - new_ref) * acc_ref[2 * t + mp] + _dot(vt, p)
            excess[t] = jnp.maximum(excess[t], bm - new_ref)
        ref[t][mp] = new_ref
        seen[t][mp].append(bm)
        if idx == nk - 1 and mp == 1:
            finalize(t)

    for t in range(tiles):
        @pl.when(jnp.max(excess[t]) > ATTN_MAX_EXCESS)
        def _exact_tile(t=t):
            m_ref[...] = jnp.full(m_ref.shape, NEG_BIG, F32)
            for mp in range(2):
                acc_ref[2 * t + mp] = jnp.zeros((ATTN_VROWS, tq), F32)

            def body(j, carry):
                side = jnp.where(j < jds[t], 0, jnp.where(j == jds[t], 2, 1))
                ka, vt, r = ka_ref[j], vt_ref[j], r_ref[3 * t + side]
                bias = jnp.where(j == jds[t], c_f32, 0.0) * diag_bias(t, j)
                for mp in range(2):
                    s_t = _dot(ka, latched(t, mp, side)) - bias
                    m_old = m_ref[mp]
                    m_new = jnp.maximum(m_old, col_max(s_t) + r)
                    p = jnp.exp2(s_t - (m_new - r)).astype(BF16)
                    acc_ref[2 * t + mp] = jnp.exp2(m_old - m_new) * acc_ref[2 * t + mp] + _dot(vt, p)
                    m_ref[mp] = m_new
                return carry
            lax.fori_loop(0, nk, body, 0)
            finalize(t)


def _attn(proj, lq1, lk1, lq2, lk2, g_col, col0, lambda_init):
    B, S, _ = proj.shape
    tq, tk = min(ATTN_TQ, S), min(ATTN_TK, S)
    nk = S // tk
    tiles = max(1, min(ATTN_STREAM_BLOCKS // nk, S // tq))
    cb = col0 // LANES
    vec = pl.BlockSpec((1, LANES // 2), lambda b, h, i: (0, 0))
    return pl.pallas_call(
        functools.partial(_attn_kernel, seq=S, tq=tq, tk=tk, tiles=tiles, lambda_init=lambda_init),
        grid=(B, DIFF_HEADS, S // (tiles * tq)),
        in_specs=[
            pl.BlockSpec((None, tiles * tq, LANES), lambda b, h, i: (b, i, cb + h)),
            pl.BlockSpec((None, S, LANES), lambda b, h, i: (b, 0, cb + DIFF_HEADS + h)),
            pl.BlockSpec((None, S, LANES), lambda b, h, i: (b, 0, cb + 2 * DIFF_HEADS + h)),
            vec, vec, vec, vec,
            pl.BlockSpec((LANES, 1), lambda b, h, i: (0, 0)),
        ],
        out_specs=pl.BlockSpec((None, tiles * tq, LANES), lambda b, h, i: (b, i, h)),
        out_shape=jax.ShapeDtypeStruct((B, S, DIFF_HEADS * LANES), BF16),
        scratch_shapes=[
            pltpu.VMEM((nk, tk, 2 * LANES), BF16),
            pltpu.VMEM((nk, ATTN_VROWS, tk), BF16),
            pltpu.VMEM((2 * tiles, LANES, tq), BF16),
            pltpu.VMEM((3, LANES, tq), BF16),
            pltpu.VMEM((2 * tiles, ATTN_VROWS, tq), F32),
            pltpu.VMEM((2, 1, tq), F32),
            pltpu.VMEM((3 * tiles, 1, tq), F32),
        ],
        compiler_params=_params(3),
        name="attn",
    )(proj, proj, proj, lq1, lk1, lq2, lk2, g_col)


def _outproj_kernel(hf_ref, hb_ref, mo_ref, oa_ref, x_ref, gm_ref, w_ref, gp_ref, out_ref, *, hd):
    width = MLSTM_HEADS * hd
    ys = []
    for h in range(MLSTM_HEADS):
        sl = slice(h * hd, (h + 1) * hd)
        hm = _rms(hf_ref[:, sl].astype(F32) + hb_ref[:, sl].astype(F32), gm_ref[:, sl])
        ys.append((jax.nn.sigmoid(mo_ref[:, sl].astype(F32)) * hm).astype(BF16))
    ym = jnp.concatenate(ys, axis=1)
    mix = _dot(ym, w_ref[0:width, :]) + _dot(oa_ref[...], w_ref[width:, :])
    out_ref[...] = x_ref[...] + _rms(mix, gp_ref[...])


def _out_proj(hf, hb, proj, oa, x, gm, w_out, gp, width):
    B, S, D = x.shape
    tm = min(512, S)
    row = lambda b, i: (b, i, 0)
    const = lambda b, i: (0, 0)
    return pl.pallas_call(
        functools.partial(_outproj_kernel, hd=width // MLSTM_HEADS),
        grid=(B, S // tm),
        in_specs=[
            pl.BlockSpec((None, tm, width), row),
            pl.BlockSpec((None, tm, width), row),
            pl.BlockSpec((None, tm, width), lambda b, i: (b, i, 3)),
            pl.BlockSpec((None, tm, oa.shape[2]), row),
            pl.BlockSpec((None, tm, D), row),
            pl.BlockSpec((1, width), const),
            pl.BlockSpec(w_out.shape, const),
            pl.BlockSpec((1, D), const),
        ],
        out_specs=pl.BlockSpec((None, tm, D), row),
        out_shape=jax.ShapeDtypeStruct((B, S, D), F32),
        compiler_params=_params(2),
        name="out_proj",
    )(hf, hb, proj, oa, x, gm, w_out, gp)


def _mlp_kernel(x_ref, gpre_ref, wup_ref, wdown_ref, gpost_ref, out_ref, h_ref, acc_ref):
    f = pl.program_id(2)

    @pl.when(f == 0)
    def _():
        h_ref[...] = _rms(x_ref[...], gpre_ref[...]).astype(BF16)
        acc_ref[...] = jnp.zeros_like(acc_ref)

    nch = wup_ref.shape[1] // MLP_CHUNK

    def up(c):
        u = jnp.maximum(_dot(h_ref[...], wup_ref[:, c * MLP_CHUNK:(c + 1) * MLP_CHUNK]), 0.0)
        return (u * u).astype(BF16)

    pending = [up(0)]
    for c in range(nch):
        u = pending.pop(0)
        if c + 1 < nch:
            pending.append(up(c + 1))
        acc_ref[...] += _dot(u, wdown_ref[c * MLP_CHUNK:(c + 1) * MLP_CHUNK, :])

    @pl.when(f == pl.num_programs(2) - 1)
    def _():
        out_ref[...] = x_ref[...] + _rms(acc_ref[...], gpost_ref[...])


def _mlp(x, gpre, w_up, w_down, gpost):
    B, S, D = x.shape
    F = w_up.shape[1]
    tm = min(512, S)
    tf = 2048
    row = lambda b, i, f: (b, i, 0)
    const = lambda b, i, f: (0, 0)
    return pl.pallas_call(
        _mlp_kernel,
        grid=(B, S // tm, F // tf),
        in_specs=[
            pl.BlockSpec((None, tm, D), row, pipeline_mode=pl.Buffered(1)),
            pl.BlockSpec((1, D), const),
            pl.BlockSpec((D, tf), lambda b, i, f: (0, f)),
            pl.BlockSpec((tf, D), lambda b, i, f: (f, 0)),
            pl.BlockSpec((1, D), const),
        ],
        out_specs=pl.BlockSpec((None, tm, D), row, pipeline_mode=pl.Buffered(1)),
        out_shape=jax.ShapeDtypeStruct((B, S, D), F32),
        scratch_shapes=[pltpu.VMEM((tm, D), BF16), pltpu.VMEM((tm, D), F32)],
        compiler_params=_params(3),
        name="mlp",
    )(x, gpre, w_up, w_down, gpost)


def _layer(x, lambda_init, g_pre_mix, w_in, b_gates, mlstm_norm_g, lq1, lk1, lq2, lk2, subln_g,
           w_out, g_post_mix, g_pre_mlp, w_up, w_down, g_post_mlp):
    D = x.shape[-1]
    width = D // 2
    gate0 = 4 * width
    w_bf = w_in.astype(BF16)
    w_main = jnp.concatenate([w_bf[:, :gate0], w_bf[:, gate0 + N_GATES:]], axis=1)
    w_gt = w_bf[:, gate0:gate0 + N_GATES].T
    b_t = b_gates.reshape(N_GATES, 1)
    row = lambda v: v.reshape(1, -1)

    proj, lg, lgt = _in_proj(x, row(g_pre_mix), w_main, w_gt, b_t)
    hf, hb = _mlstm(proj, lg, lgt, width)
    oa = _attn(proj, row(lq1), row(lk1), row(lq2), row(lk2), subln_g.reshape(-1, 1), gate0, lambda_init)
    x1 = _out_proj(hf, hb, proj, oa, x, row(mlstm_norm_g), w_out.astype(BF16), row(g_post_mix), width)
    return _mlp(x1, row(g_pre_mlp), w_up.astype(BF16), w_down.astype(BF16), row(g_post_mlp))


def _trunk(x, *weights):
    depth = weights[0].shape[0]
    for l in range(depth):
        lambda_init = 0.8 - 0.6 * math.exp(-0.3 * l)
        x = _layer(x, lambda_init, *[w[l] for w in weights])
    return x


def kernel(x_prompt, x_sample, g_pre_mix, w_in, b_gates, mlstm_norm_g, lambda_q1, lambda_k1, lambda_q2,
           lambda_k2, subln_g, w_out, g_post_mix, g_pre_mlp, w_up, w_down, g_post_mlp):
    weights = (g_pre_mix, w_in, b_gates, mlstm_norm_g, lambda_q1, lambda_k1, lambda_q2, lambda_k2,
               subln_g, w_out, g_post_mix, g_pre_mlp, w_up, w_down, g_post_mlp)
    return (_trunk(x_prompt, *weights), _trunk(x_sample, *weights))
```
